```python
import math
import jax, jax.numpy as jnp
from jax import lax
import numpy as np

D_MODEL = 1024
BATCH = 8
SEQ = 4096
DEPTH = 2

GRID_W = 64
CTX_LEN = 256
Q_BLOCK = 128
ROPE_THETA = 10000.0
NORM_EPS = 1e-6

CONV_CH = 256
CONV_WIDTH = 3
MLA_HEADS = 4
MLA_Q_LORA = 256
MLA_KV_LORA = 128
MLA_NOPE = 64
MLA_ROPE = 32
MLA_V = 64
DIFF_HEADS = 4
DIFF_QK = 32
DIFF_V = 64
GQA_Q_HEADS = 4
GQA_KV_HEADS = 2
GQA_GROUP = GQA_Q_HEADS // GQA_KV_HEADS
GQA_HD = 64

IN_CONV = 3 * CONV_CH
IN_MLA = MLA_Q_LORA + MLA_KV_LORA + MLA_ROPE
DIFF_QW = DIFF_HEADS * 2 * DIFF_QK
IN_DIFF = 2 * DIFF_QW + DIFF_HEADS * DIFF_V
GQA_QW = GQA_Q_HEADS * GQA_HD
GQA_KW = GQA_KV_HEADS * GQA_HD
IN_GQA = GQA_QW + 2 * GQA_KW
D_IN = IN_CONV + IN_MLA + IN_DIFF + IN_GQA
IN_SPLITS = (IN_CONV, IN_CONV + IN_MLA, IN_CONV + IN_MLA + IN_DIFF)
D_MIX = CONV_CH + MLA_HEADS * MLA_V + DIFF_HEADS * DIFF_V + GQA_Q_HEADS * GQA_HD

MLA_SCALE = (MLA_NOPE + MLA_ROPE) ** -0.5
DIFF_SCALE = DIFF_QK ** -0.5
GQA_SCALE = GQA_HD ** -0.5

N_GROUPS = 4
EXPERTS_PER_GROUP = 8
N_EXPERTS = N_GROUPS * EXPERTS_PER_GROUP
TOP_K = 2
D_EXPERT = 256
EXPERT_BLOCK = 128

DEEPNORM_ALPHA = (2 * DEPTH) ** 0.25
DEEPNORM_BETA = (8 * DEPTH) ** -0.25
ADA_SCALE = 0.5

kernel_name = "hybrid_diffusion_trunk_mla_diff_gqa_conv_hmoe"


def rms_norm(z, gain):
    zf = z.astype(jnp.float32)
    zf = zf * lax.rsqrt(jnp.mean(zf * zf, axis=-1, keepdims=True) + NORM_EPS)
    return (zf * gain.astype(jnp.float32)).astype(z.dtype)


def layer_norm(z, gain, bias):
    zf = z.astype(jnp.float32)
    mu = jnp.mean(zf, axis=-1, keepdims=True)
    var = jnp.mean(jnp.square(zf - mu), axis=-1, keepdims=True)
    out = (zf - mu) * lax.rsqrt(var + NORM_EPS) * gain.astype(jnp.float32) + bias.astype(jnp.float32)
    return out.astype(z.dtype)


def rope_1d(z, pos):
    half = z.shape[-1] // 2
    freqs = ROPE_THETA ** (-jnp.arange(half, dtype=jnp.float32) / half)
    ang = pos.astype(jnp.float32)[:, None] * freqs
    shape = (ang.shape[0],) + (1,) * (z.ndim - 3) + (half,)
    cos = jnp.cos(ang).reshape(shape).astype(z.dtype)
    sin = jnp.sin(ang).reshape(shape).astype(z.dtype)
    z1, z2 = z[..., :half], z[..., half:]
    return jnp.concatenate([z1 * cos - z2 * sin, z2 * cos + z1 * sin], axis=-1)


def rope2d(z, row, col):
    h = z.shape[-1] // 2
    return jnp.concatenate([rope_1d(z[..., :h], row), rope_1d(z[..., h:], col)], axis=-1)


def softmax_f32(scores, scale):
    return jax.nn.softmax(scores.astype(jnp.float32) * scale, axis=-1)


def sweep_query_blocks(fn, q):
    b, s = q.shape[:2]
    nb = s // Q_BLOCK
    qb = jnp.moveaxis(q.reshape((b, nb, Q_BLOCK) + q.shape[2:]), 1, 0)
    out = lax.map(fn, qb)
    return jnp.moveaxis(out, 0, 1).reshape((b, s) + out.shape[3:])


def mha_core(q, k, v, scale):
    p = softmax_f32(jnp.einsum('bqhd,bthd->bhqt', q, k), scale)
    return jnp.einsum('bhqt,bthd->bqhd', p.astype(v.dtype), v)


def gqa_core(q, k, v, scale):
    p = softmax_f32(jnp.einsum('bqkgd,btkd->bkgqt', q, k), scale)
    return jnp.einsum('bkgqt,btkd->bqkgd', p.astype(v.dtype), v)


def diff_core(q, k, v, lam, scale):
    p = softmax_f32(jnp.einsum('bqhmd,bthmd->bhmqt', q, k), scale)
    a = p[:, :, 0] - lam * p[:, :, 1]
    return jnp.einsum('bhqt,bthd->bqhd', a.astype(v.dtype), v)


def conv_group(z, conv_w):
    gate_b, gate_c, val = jnp.split(z, 3, axis=-1)
    zc = gate_c * val
    conv = lax.conv_general_dilated(zc, conv_w[:, None, :], window_strides=(1,),
                                    padding=((CONV_WIDTH // 2, CONV_WIDTH // 2),),
                                    dimension_numbers=('NWC', 'WIO', 'NWC'),
                                    feature_group_count=CONV_CH)
    return gate_b * conv


def mla_queries(z, q_norm, w_uq, w_qr, pos):
    b, n = z.shape[:2]
    cq = rms_norm(z[..., :MLA_Q_LORA], q_norm)
    q_nope = (cq @ w_uq).reshape(b, n, MLA_HEADS, MLA_NOPE)
    q_rope = (cq @ w_qr).reshape(b, n, MLA_HEADS, MLA_ROPE)
    if pos is not None:
        q_rope = rope2d(q_rope, *pos)
    return jnp.concatenate([q_nope, q_rope], axis=-1)


def mla_keys_values(z, kv_norm, w_uk, w_uv, pos):
    b, n = z.shape[:2]
    ckv = rms_norm(z[..., MLA_Q_LORA:MLA_Q_LORA + MLA_KV_LORA], kv_norm)
    k_rope = z[..., MLA_Q_LORA + MLA_KV_LORA:][:, :, None, :]
    if pos is not None:
        k_rope = rope2d(k_rope, *pos)
    k_nope = (ckv @ w_uk).reshape(b, n, MLA_HEADS, MLA_NOPE)
    v = (ckv @ w_uv).reshape(b, n, MLA_HEADS, MLA_V)
    k = jnp.concatenate([k_nope, jnp.broadcast_to(k_rope, (b, n, MLA_HEADS, MLA_ROPE))], axis=-1)
    return k, v


def diff_queries(z, pos):
    b, n = z.shape[:2]
    q = z[..., :DIFF_QW].reshape(b, n, DIFF_HEADS, 2, DIFF_QK)
    return rope2d(q, *pos) if pos is not None else q


def diff_keys_values(z, pos):
    b, n = z.shape[:2]
    k = z[..., DIFF_QW:2 * DIFF_QW].reshape(b, n, DIFF_HEADS, 2, DIFF_QK)
    v = z[..., 2 * DIFF_QW:].reshape(b, n, DIFF_HEADS, DIFF_V)
    if pos is not None:
        k = rope2d(k, *pos)
    return k, v


def gqa_queries(z, q_gain, pos):
    b, n = z.shape[:2]
    q = rms_norm(z[..., :GQA_QW].reshape(b, n, GQA_KV_HEADS, GQA_GROUP, GQA_HD), q_gain)
    return rope2d(q, *pos) if pos is not None else q


def gqa_keys_values(z, k_gain, pos):
    b, n = z.shape[:2]
    k = rms_norm(z[..., GQA_QW:GQA_QW + GQA_KW].reshape(b, n, GQA_KV_HEADS, GQA_HD), k_gain)
    v = z[..., GQA_QW + GQA_KW:].reshape(b, n, GQA_KV_HEADS, GQA_HD)
    if pos is not None:
        k = rope2d(k, *pos)
    return k, v


def hybrid_mixer(u, uc, pos, conv_w, mla_q_norm, mla_kv_norm, mla_w_uq, mla_w_qr, mla_w_uk, mla_w_uv,
                 diff_lambda, diff_subln, gqa_q_norm, gqa_k_norm, layer_idx, need_ctx):
    b, s = u.shape[:2]
    za, zm, zd, zg = jnp.split(u, IN_SPLITS, axis=-1)
    zac, zmc, zdc, zgc = jnp.split(uc, IN_SPLITS, axis=-1)

    k_mc, v_mc = mla_keys_values(zmc, mla_kv_norm, mla_w_uk, mla_w_uv, None)
    k_ml, v_ml = mla_keys_values(zm, mla_kv_norm, mla_w_uk, mla_w_uv, pos)
    k_m = jnp.concatenate([k_mc, k_ml], axis=1)
    v_m = jnp.concatenate([v_mc, v_ml], axis=1)
    y_m = sweep_query_blocks(lambda qb: mha_core(qb, k_m, v_m, MLA_SCALE),
                             mla_queries(zm, mla_q_norm, mla_w_uq, mla_w_qr, pos))

    lam_init = 0.8 - 0.6 * math.exp(-0.3 * layer_idx)
    dl = diff_lambda.astype(jnp.float32)
    lam = jnp.exp(jnp.sum(dl[0] * dl[1])) - jnp.exp(jnp.sum(dl[2] * dl[3])) + lam_init
    k_dc, v_dc = diff_keys_values(zdc, None)
    k_dl, v_dl = diff_keys_values(zd, pos)
    k_d = jnp.concatenate([k_dc, k_dl], axis=1)
    v_d = jnp.concatenate([v_dc, v_dl], axis=1)
    y_d = sweep_query_blocks(lambda qb: diff_core(qb, k_d, v_d, lam, DIFF_SCALE), diff_queries(zd, pos))

    def diff_out(o):
        o = rms_norm(o, diff_subln) * (1.0 - lam_init)
        return o.reshape(o.shape[0], o.shape[1], DIFF_HEADS * DIFF_V)

    k_gc, v_gc = gqa_keys_values(zgc, gqa_k_norm, None)
    k_gl, v_gl = gqa_keys_values(zg, gqa_k_norm, pos)
    k_g = jnp.concatenate([k_gc, k_gl], axis=1)
    v_g = jnp.concatenate([v_gc, v_gl], axis=1)
    y_g = sweep_query_blocks(lambda qb: gqa_core(qb, k_g, v_g, GQA_SCALE), gqa_queries(zg, gqa_q_norm, pos))

    y = jnp.concatenate([conv_group(za, conv_w),
                         y_m.reshape(b, s, MLA_HEADS * MLA_V),
                         diff_out(y_d),
                         y_g.reshape(b, s, GQA_Q_HEADS * GQA_HD)], axis=-1)
    if not need_ctx:
        return y, None

    bc, nc = uc.shape[:2]
    yc_m = mha_core(mla_queries(zmc, mla_q_norm, mla_w_uq, mla_w_qr, None), k_mc, v_mc, MLA_SCALE)
    yc_d = diff_core(diff_queries(zdc, None), k_dc, v_dc, lam, DIFF_SCALE)
    yc_g = gqa_core(gqa_queries(zgc, gqa_q_norm, None), k_gc, v_gc, GQA_SCALE)
    yc = jnp.concatenate([conv_group(zac, conv_w),
                          yc_m.reshape(bc, nc, MLA_HEADS * MLA_V),
                          diff_out(yc_d),
                          yc_g.reshape(bc, nc, GQA_Q_HEADS * GQA_HD)], axis=-1)
    return y, yc


def grouped_expert_ffn(h, flat_e, flat_w, w1, w3, w2):
    n, d = h.shape
    a = flat_e.shape[0]
    k = a // n
    flat_tok = jnp.repeat(jnp.arange(n, dtype=jnp.int32), k)
    order = jnp.argsort(flat_e)
    se = flat_e[order]
    counts = jnp.bincount(flat_e, length=N_EXPERTS)
    starts = jnp.cumsum(counts) - counts
    padded = (counts + EXPERT_BLOCK - 1) // EXPERT_BLOCK * EXPERT_BLOCK
    pad_ends = jnp.cumsum(padded)
    pad_starts = pad_ends - padded
    dest = pad_starts[se] + (jnp.arange(a, dtype=jnp.int32) - starts[se])
    n_slots = (-(-a // EXPERT_BLOCK) + N_EXPERTS) * EXPERT_BLOCK
    slot_tok = jnp.full((n_slots,), n, jnp.int32).at[dest].set(flat_tok[order])
    slot_w = jnp.zeros((n_slots,), flat_w.dtype).at[dest].set(flat_w[order])
    xs = jnp.take(h, slot_tok, axis=0, mode='fill', fill_value=0)
    nb = n_slots // EXPERT_BLOCK
    block_e = jnp.minimum(jnp.searchsorted(pad_ends, jnp.arange(nb) * EXPERT_BLOCK, side='right'),
                          N_EXPERTS - 1)

    def expert_block(args):
        xb, e = args
        return (jax.nn.silu(xb @ w1[e]) * (xb @ w3[e])) @ w2[e]

    ys = lax.map(expert_block, (xs.reshape(nb, EXPERT_BLOCK, d), block_e)).reshape(n_slots, d)
    ys = ys * slot_w[:, None].astype(ys.dtype)
    return jax.ops.segment_sum(ys, slot_tok, num_segments=n)


def hier_moe(h, w_group, b_group, w_sub, b_sub, w1, w3, w2):
    n = h.shape[0]
    pg = jax.nn.softmax((h @ w_group + b_group).astype(jnp.float32), axis=-1)
    g_w, g_idx = lax.top_k(pg, 1)
    sub = (h @ w_sub + b_sub).reshape(n, N_GROUPS, EXPERTS_PER_GROUP)
    sub = sub[jnp.arange(n), g_idx[:, 0]]
    ps = jax.nn.softmax(sub.astype(jnp.float32), axis=-1)
    e_w, e_local = lax.top_k(ps, TOP_K)
    e_w = e_w / jnp.sum(e_w, axis=-1, keepdims=True)
    weights = g_w * e_w
    experts = g_idx * EXPERTS_PER_GROUP + e_local
    return grouped_expert_ffn(h, experts.reshape(-1), weights.reshape(-1), w1, w3, w2)


def setup_inputs(seed: int = 0) -> dict:
    key = jax.random.key(seed)
    ks = iter(jax.random.split(key, 40))
    L, D = DEPTH, D_MODEL

    def nrm(shape, scale):
        return jax.random.normal(next(ks), shape, jnp.float32) * scale

    return {
        "x": nrm((BATCH, SEQ, D), 1.0),
        "c": nrm((BATCH, D), 1.0),
        "ctx": nrm((BATCH, CTX_LEN, D), 1.0),
        "c_ctx": nrm((D,), 1.0),
        "w_ada": nrm((L, D, 6 * D), ADA_SCALE * D ** -0.5),
        "b_ada": nrm((L, 6 * D), 0.01),
        "w_in": nrm((L, D, D_IN), D ** -0.5),
        "w_out": nrm((L, D_MIX, D), DEEPNORM_BETA * D_MIX ** -0.5),
        "conv_w": nrm((L, CONV_WIDTH, CONV_CH), CONV_WIDTH ** -0.5),
        "mla_q_norm": 1.0 + nrm((L, MLA_Q_LORA), 0.02),
        "mla_kv_norm": 1.0 + nrm((L, MLA_KV_LORA), 0.02),
        "mla_w_uq": nrm((L, MLA_Q_LORA, MLA_HEADS * MLA_NOPE), MLA_Q_LORA ** -0.5),
        "mla_w_qr": nrm((L, MLA_Q_LORA, MLA_HEADS * MLA_ROPE), MLA_Q_LORA ** -0.5),
        "mla_w_uk": nrm((L, MLA_KV_LORA, MLA_HEADS * MLA_NOPE), MLA_KV_LORA ** -0.5),
        "mla_w_uv": nrm((L, MLA_KV_LORA, MLA_HEADS * MLA_V), MLA_KV_LORA ** -0.5),
        "diff_lambda": nrm((L, 4, DIFF_QK), 0.1),
        "diff_subln": 1.0 + nrm((L, DIFF_V), 0.02),
        "gqa_q_norm": 1.0 + nrm((L, GQA_HD), 0.02),
        "gqa_k_norm": 1.0 + nrm((L, GQA_HD), 0.02),
        "ln1_g": 1.0 + nrm((L, D), 0.02),
        "ln1_b": nrm((L, D), 0.02),
        "ln2_g": 1.0 + nrm((L, D), 0.02),
        "ln2_b": nrm((L, D), 0.02),
        "moe_w_group": nrm((L, D, N_GROUPS), D ** -0.5),
        "moe_b_group": nrm((L, N_GROUPS), 0.01),
        "moe_w_sub": nrm((L, D, N_EXPERTS), D ** -0.5),
        "moe_b_sub": nrm((L, N_EXPERTS), 0.01),
        "moe_w1": nrm((L, N_EXPERTS, D, D_EXPERT), D ** -0.5),
        "moe_w3": nrm((L, N_EXPERTS, D, D_EXPERT), D ** -0.5),
        "moe_w2": nrm((L, N_EXPERTS, D_EXPERT, D), DEEPNORM_BETA * D_EXPERT ** -0.5),
    }


def reference(x, c, ctx, c_ctx, w_ada, b_ada, w_in, w_out, conv_w, mla_q_norm, mla_kv_norm,
              mla_w_uq, mla_w_qr, mla_w_uk, mla_w_uv, diff_lambda, diff_subln, gqa_q_norm, gqa_k_norm,
              ln1_g, ln1_b, ln2_g, ln2_b, moe_w_group, moe_b_group, moe_w_sub, moe_b_sub,
              moe_w1, moe_w3, moe_w2):
    b, s, d = x.shape
    n_ctx = ctx.shape[1]
    rows = s // GRID_W
    row = jnp.repeat(jnp.arange(rows, dtype=jnp.int32), GRID_W)
    col = jnp.tile(jnp.arange(GRID_W, dtype=jnp.int32), rows)
    pos = (row, col)
    silu_c = jax.nn.silu(c)
    silu_cc = jax.nn.silu(c_ctx)
    cx = ctx
    for l in range(DEPTH):
        need_ctx = l < DEPTH - 1
        mod = (silu_c @ w_ada[l] + b_ada[l])[:, None, :]
        sh1, sc1, g1, sh2, sc2, g2 = jnp.split(mod, 6, axis=-1)
        modc = silu_cc @ w_ada[l] + b_ada[l]
        csh1, csc1, cg1, csh2, csc2, cg2 = jnp.split(modc, 6, axis=-1)

        u = (x * (1.0 + sc1) + sh1) @ w_in[l]
        uc = (cx * (1.0 + csc1) + csh1) @ w_in[l]
        y, yc = hybrid_mixer(u, uc, pos, conv_w[l], mla_q_norm[l], mla_kv_norm[l], mla_w_uq[l],
                             mla_w_qr[l], mla_w_uk[l], mla_w_uv[l], diff_lambda[l], diff_subln[l],
                             gqa_q_norm[l], gqa_k_norm[l], l, need_ctx)
        x = layer_norm(DEEPNORM_ALPHA * x + g1 * (y @ w_out[l]), ln1_g[l], ln1_b[l])
        tokens = (x * (1.0 + sc2) + sh2).reshape(b * s, d)
        if need_ctx:
            cx = layer_norm(DEEPNORM_ALPHA * cx + cg1 * (yc @ w_out[l]), ln1_g[l], ln1_b[l])
            tokens = jnp.concatenate([tokens, (cx * (1.0 + csc2) + csh2).reshape(b * n_ctx, d)], axis=0)

        f = hier_moe(tokens, moe_w_group[l], moe_b_group[l], moe_w_sub[l], moe_b_sub[l],
                     moe_w1[l], moe_w3[l], moe_w2[l])
        x = layer_norm(DEEPNORM_ALPHA * x + g2 * f[:b * s].reshape(b, s, d), ln2_g[l], ln2_b[l])
        if need_ctx:
            cx = layer_norm(DEEPNORM_ALPHA * cx + cg2 * f[b * s:].reshape(b, n_ctx, d), ln2_g[l], ln2_b[l])
    return x
```

```python
import functools
import math

import jax
import jax.numpy as jnp
import numpy as np
from jax import lax
from jax.experimental import pallas as pl
from jax.experimental.pallas import tpu as pltpu

F32 = jnp.float32
BF16 = jnp.bfloat16

GRID_W = 64
ROPE_THETA = 10000.0
NORM_EPS = 1e-6
CONV_CH = 256
MLA_HEADS, MLA_Q_LORA, MLA_KV_LORA, MLA_NOPE, MLA_ROPE, MLA_V = 4, 256, 128, 64, 32, 64
DIFF_HEADS, DIFF_QK, DIFF_V = 4, 32, 64
GQA_Q_HEADS, GQA_KV_HEADS, GQA_HD = 4, 2, 64
IN_CONV = 3 * CONV_CH
IN_MLA = MLA_Q_LORA + MLA_KV_LORA + MLA_ROPE
DIFF_QW = DIFF_HEADS * 2 * DIFF_QK
IN_DIFF = 2 * DIFF_QW + DIFF_HEADS * DIFF_V
GQA_QW = GQA_Q_HEADS * GQA_HD
GQA_KW = GQA_KV_HEADS * GQA_HD
OFF_MLA = IN_CONV
OFF_DIFF = OFF_MLA + IN_MLA
OFF_GQA = OFF_DIFF + IN_DIFF
MLA_SCALE = (MLA_NOPE + MLA_ROPE) ** -0.5
DIFF_SCALE = DIFF_QK ** -0.5
GQA_SCALE = GQA_HD ** -0.5
N_GROUPS, EXPERTS_PER_GROUP = 4, 8
N_EXPERTS = N_GROUPS * EXPERTS_PER_GROUP
D_EXPERT = 256
LOG2E = math.log2(math.e)

LANES = 128
SUBLANES = 8
TM = 256
EB = 256
VMEM_LIMIT = 56 * 1024 * 1024

P_CONV = 0
P_MLA = P_CONV + 768
P_DIFF = P_MLA + 512
P_GQA = P_DIFF + 1024
P_END = P_GQA + 896


def _cparams(sem):
    return pltpu.CompilerParams(dimension_semantics=sem, vmem_limit_bytes=VMEM_LIMIT)


def _lane(shape):
    return lax.broadcasted_iota(jnp.int32, shape, len(shape) - 1)


def _mod_kernel(c_ref, w_ref, b_ref, o_ref):
    c = c_ref[...]
    sc = (c * jax.nn.sigmoid(c)).astype(BF16)
    o_ref[0] = jnp.dot(sc, w_ref[0].astype(BF16), preferred_element_type=F32) + b_ref[0]


def _modulation(cc, w_ada, b_ada):
    n_layers, d, d6 = w_ada.shape
    r = cc.shape[0]
    return pl.pallas_call(
        _mod_kernel,
        grid=(n_layers, d6 // d),
        in_specs=[pl.BlockSpec((r, d), lambda l, n: (0, 0)),
                  pl.BlockSpec((1, d, d), lambda l, n: (l, 0, n)),
                  pl.BlockSpec((1, 1, d), lambda l, n: (l, 0, n))],
        out_specs=pl.BlockSpec((1, r, d), lambda l, n: (l, 0, n)),
        out_shape=jax.ShapeDtypeStruct((n_layers, r, d6), F32),
        compiler_params=_cparams(("arbitrary", "arbitrary")),
        name="mod",
    )(cc, w_ada, b_ada.reshape(n_layers, 1, d6))


def _rope(z, cos, sin_signed, half):
    n = z.shape[-1]
    first = (_lane(z.shape) % (2 * half)) < half
    partner = jnp.where(first, pltpu.roll(z, n - half, 1), pltpu.roll(z, half, 1))
    return z * cos + partner * sin_signed


def _rms(z, gain):
    return z * lax.rsqrt(jnp.mean(z * z, axis=-1, keepdims=True) + NORM_EPS) * gain


def _seg_rms64(z, gain):
    lo = _lane(z.shape) < 64
    z2 = z * z
    s_lo = jnp.sum(jnp.where(lo, z2, 0.0), axis=-1, keepdims=True)
    s_hi = jnp.sum(jnp.where(lo, 0.0, z2), axis=-1, keepdims=True)
    ms = jnp.where(lo, s_lo, s_hi) * (1.0 / 64)
    return z * lax.rsqrt(ms + NORM_EPS) * gain


def _ones_lane(odd):
    return (_lane((1, LANES)) == (0 if odd else 64)).astype(F32)


def _proj_kernel(x_ref, mod_ref, win_ref, cm_ref, sm_ref, cd_ref, sd_ref, cg_ref, sg_ref,
                 qn_ref, kvn_ref, wq_ref, wkv_ref, gq_ref, gk_ref,
                 gb_ref, zc_ref, qm_ref, km_ref, vm_ref, qd_ref, kd_ref, vd_ref, qg_ref, kg_ref, vg_ref):
    x = x_ref[0]
    sh1 = mod_ref[0, 0, 0:1, :]
    sc1 = mod_ref[0, 0, 1:2, :]
    xm = (x * (1.0 + sc1) + sh1).astype(BF16)

    uc = jnp.dot(xm, win_ref[:, P_CONV:P_MLA], preferred_element_type=F32)
    gb_ref[0] = uc[:, 0:256]
    zc_ref[0] = uc[:, 256:512] * uc[:, 512:768]

    um = jnp.dot(xm, win_ref[:, P_MLA:P_DIFF], preferred_element_type=F32)
    cm, sm = cm_ref[...], sm_ref[...]
    cq = _rms(um[:, 0:256], qn_ref[...]).astype(BF16)
    qall = jnp.dot(cq, wq_ref[...], preferred_element_type=F32)
    ckv = _rms(um[:, 256:384], kvn_ref[...]).astype(BF16)
    kv = jnp.dot(ckv, wkv_ref[...], preferred_element_type=F32)
    kr = _rope(um[:, 384:512], cm, sm, 8)
    for h in range(MLA_HEADS):
        sl = slice(LANES * h, LANES * (h + 1))
        qm_ref[0, :, sl] = (_rope(qall[:, sl], cm, sm, 8) * (MLA_SCALE * LOG2E)).astype(BF16)
        km_ref[0, :, sl] = (kv[:, sl] + kr).astype(BF16)
        vm_ref[0, :, sl] = (kv[:, 512 + LANES * h:512 + LANES * (h + 1)] + _ones_lane(h % 2)).astype(BF16)

    ud = jnp.dot(xm, win_ref[:, P_DIFF:P_GQA], preferred_element_type=F32)
    cd, sd = cd_ref[...], sd_ref[...]
    for c in range(2):
        sl = slice(LANES * c, LANES * (c + 1))
        qd_ref[0, :, sl] = (_rope(ud[:, sl], cd, sd, 8) * (DIFF_SCALE * LOG2E)).astype(BF16)
        kd_ref[0, :, sl] = _rope(ud[:, 256 + LANES * c:256 + LANES * (c + 1)], cd, sd, 8).astype(BF16)
    for h in range(DIFF_HEADS):
        sl = slice(LANES * h, LANES * (h + 1))
        vd_ref[0, :, sl] = (ud[:, 512 + LANES * h:512 + LANES * (h + 1)] + _ones_lane(h % 2)).astype(BF16)

    ug = jnp.dot(xm, win_ref[:, P_GQA:P_END], preferred_element_type=F32)
    cg, sg = cg_ref[...], sg_ref[...]
    for c in range(2):
        sl = slice(LANES * c, LANES * (c + 1))
        qg_ref[0, :, sl] = (_rope(_seg_rms64(ug[:, sl], gq_ref[...]), cg, sg, 16) * (GQA_SCALE * LOG2E)).astype(BF16)
    kg_ref[0] = _rope(_seg_rms64(ug[:, 256:384], gk_ref[...]), cg, sg, 16).astype(BF16)
    for i in range(4):
        sl = slice(LANES * i, LANES * (i + 1))
        vg_ref[0, :, sl] = (ug[:, 384 + LANES * i:384 + LANES * (i + 1)] + _ones_lane(i % 2)).astype(BF16)


def _proj(xcat, modsel, win_p, tabs, qn, kvn, wq, wkv, gq, gk, nct):
    b, t, d = xcat.shape
    nt = t // TM

    def tile(width):
        return pl.BlockSpec((1, TM, width), lambda bi, j: (bi, j, 0))

    def full(arr):
        return pl.BlockSpec(arr.shape, lambda bi, j: (0,) * arr.ndim)

    tab_spec = pl.BlockSpec((TM, LANES), lambda bi, j: (j, 0))
    widths = [(256, F32), (256, F32), (512, BF16), (512, BF16), (512, BF16), (256, BF16), (256, BF16),
              (512, BF16), (256, BF16), (128, BF16), (512, BF16)]
    return pl.pallas_call(
        _proj_kernel,
        grid=(b, nt),
        in_specs=[tile(d),
                  pl.BlockSpec((1, 1, 6, d), lambda bi, j: (bi, jnp.minimum(j // nct, 1), 0, 0)),
                  full(win_p)] + [tab_spec] * 6 + [full(a) for a in (qn, kvn, wq, wkv, gq, gk)],
        out_specs=[tile(w) for w, _ in widths],
        out_shape=[jax.ShapeDtypeStruct((b, t, w), dt) for w, dt in widths],
        compiler_params=_cparams(("arbitrary", "arbitrary")),
        name="proj",
    )(xcat, modsel, win_p, *tabs, qn, kvn, wq, wkv, gq, gk)


def _attend(q, k, v):
    s = lax.dot_general(q, k, (((1,), (1,)), ((), ())), preferred_element_type=F32)
    m = jnp.max(s, axis=-1, keepdims=True)
    p = jnp.exp2(s - m).astype(BF16)
    return jnp.dot(p, v, preferred_element_type=F32)


def _normalise(o, odd):
    den = o[:, 0:1] if odd else o[:, 64:65]
    return o * (1.0 / den)


def _pair(even, odd):
    return jnp.where(_lane(even.shape) < 64, even, odd)


def _mla_maps():
    return [dict(qc=h, mask=None, kc=h, vc=h) for h in range(MLA_HEADS)]


def _mla_finish(outs, extra):
    o = [_normalise(outs[h], h % 2) for h in range(MLA_HEADS)]
    return jnp.concatenate([_pair(o[0], o[1]), _pair(o[2], o[3])], axis=-1)


def _gqa_maps():
    maps = []
    for kvh in range(GQA_KV_HEADS):
        for g in range(GQA_Q_HEADS // GQA_KV_HEADS):
            maps.append(dict(qc=g, mask=(64 * kvh, 64 * kvh + 64), kc=0, vc=2 * kvh + g))
    return maps


def _gqa_finish(outs, extra):
    o = [_normalise(outs[i], i % 2) for i in range(4)]
    return jnp.concatenate([_pair(o[0], o[1]), _pair(o[2], o[3])], axis=-1)


def _diff_maps():
    maps = []
    for h in range(DIFF_HEADS):
        for m in range(2):
            v = 2 * h + m
            maps.append(dict(qc=v // 4, mask=(32 * (v % 4), 32 * (v % 4) + 32), kc=v // 4, vc=h))
    return maps


def _diff_finish(lam_init, outs, extra):
    dl_ref, subln_ref = extra
    dl = dl_ref[...]
    lam = (jnp.exp(jnp.sum(dl[0:1] * dl[1:2], axis=-1, keepdims=True))
           - jnp.exp(jnp.sum(dl[2:3] * dl[3:4], axis=-1, keepdims=True)) + lam_init)
    d = []
    for h in range(DIFF_HEADS):
        d.append(_normalise(outs[2 * h], h % 2) - lam * _normalise(outs[2 * h + 1], h % 2))
    chunks = [_seg_rms64(_pair(d[2 * c], d[2 * c + 1]), subln_ref[...]) * (1.0 - lam_init) for c in range(2)]
    return jnp.concatenate(chunks, axis=-1)


def _attn_kernel(*refs, maps, finish, joff, nct, ctx_len, total_len):
    q_ref, k_ref, v_ref = refs[:3]
    extra, o_ref = refs[3:-1], refs[-1]
    jj = pl.program_id(1) + joff

    def run(tk):
        outs = []
        for mp in maps:
            q = q_ref[0, :, LANES * mp["qc"]:LANES * (mp["qc"] + 1)]
            if mp["mask"] is not None:
                lane = _lane(q.shape)
                q = jnp.where((lane >= mp["mask"][0]) & (lane < mp["mask"][1]), q, jnp.zeros_like(q))
            k = k_ref[0, 0:tk, LANES * mp["kc"]:LANES * (mp["kc"] + 1)]
            v = v_ref[0, 0:tk, LANES * mp["vc"]:LANES * (mp["vc"] + 1)]
            outs.append(_attend(q, k, v))
        o_ref[0] = finish(outs, extra).astype(o_ref.dtype)

    if joff < nct:
        pl.when(jj < nct)(lambda: run(ctx_len))
        pl.when(jj >= nct)(lambda: run(total_len))
    else:
        run(total_len)


def _attention(q, k, v, extra, maps, finish, joff, nct, name):
    b, t, _ = q.shape
    nt = t // TM

    def full(arr):
        return pl.BlockSpec(arr.shape, lambda bi, j: (0,) * arr.ndim)

    kern = functools.partial(_attn_kernel, maps=maps, finish=finish, joff=joff, nct=nct,
                             ctx_len=nct * TM, total_len=t)
    return pl.pallas_call(
        kern,
        grid=(b, nt - joff),
        in_specs=[pl.BlockSpec((1, TM, q.shape[2]), lambda bi, j: (bi, j + joff, 0)),
                  pl.BlockSpec((1, t, k.shape[2]), lambda bi, j: (bi, 0, 0)),
                  pl.BlockSpec((1, t, v.shape[2]), lambda bi, j: (bi, 0, 0))] + [full(a) for a in extra],
        out_specs=pl.BlockSpec((1, TM, 256), lambda bi, j: (bi, j, 0)),
        out_shape=jax.ShapeDtypeStruct((b, t - joff * TM, 256), BF16),
        compiler_params=_cparams(("arbitrary", "arbitrary")),
        name=name,
    )(q, k, v, *extra)


def _layer_norm(h, g, bias):
    mu = jnp.mean(h, axis=-1, keepdims=True)
    hc = h - mu
    var = jnp.mean(hc * hc, axis=-1, keepdims=True)
    return hc * lax.rsqrt(var + NORM_EPS) * g + bias


def _min_lane(cond, lane_f):
    return jnp.min(jnp.where(cond, lane_f, float(LANES)), axis=-1, keepdims=True)


def _post_kernel(x_ref, mod_ref, gb_ref, zc_ref, zp_ref, zn_ref, ym_ref, yd_ref, yg_ref, cw_ref, wout_ref,
                 g_ref, b_ref, wr_ref, br_ref, x1_ref, tok_ref, route_ref, cnt_ref, cnt_acc,
                 *, joff, nct, nt, alpha):
    jj = pl.program_id(1) + joff
    first_step = (pl.program_id(0) == 0) & (pl.program_id(1) == 0)

    @pl.when(first_step)
    def _():
        cnt_acc[...] = jnp.zeros_like(cnt_acc)

    zc = zc_ref[0]
    row = lax.broadcasted_iota(jnp.int32, zc.shape, 0)
    left_ok = (jj != 0) & (jj != nct)
    right_ok = (jj != nct - 1) & (jj != nt - 1)
    halo_prev = jnp.where(left_ok, zp_ref[0, SUBLANES - 1:SUBLANES, :], 0.0)
    halo_next = jnp.where(right_ok, zn_ref[0, 0:1, :], 0.0)
    zprev = jnp.where(row == 0, halo_prev, pltpu.roll(zc, 1, 0))
    znext = jnp.where(row == TM - 1, halo_next, pltpu.roll(zc, TM - 1, 0))
    cw = cw_ref[...]
    conv = zprev * cw[0:1] + zc * cw[1:2] + znext * cw[2:3]
    yc = (gb_ref[0] * conv).astype(BF16)

    acc = jnp.dot(yc, wout_ref[0:256, :], preferred_element_type=F32)
    acc += jnp.dot(ym_ref[0], wout_ref[256:512, :], preferred_element_type=F32)
    acc += jnp.dot(yd_ref[0], wout_ref[512:768, :], preferred_element_type=F32)
    acc += jnp.dot(yg_ref[0], wout_ref[768:1024, :], preferred_element_type=F32)

    g1 = mod_ref[0, 0, 2:3, :]
    sh2 = mod_ref[0, 0, 3:4, :]
    sc2 = mod_ref[0, 0, 4:5, :]
    x1 = _layer_norm(alpha * x_ref[0] + g1 * acc, g_ref[...], b_ref[...])
    x1_ref[0] = x1
    tok = x1 * (1.0 + sc2) + sh2
    tok_ref[0] = tok

    logits = jnp.dot(tok, wr_ref[...], preferred_element_type=F32, precision=lax.Precision.HIGHEST) + br_ref[...]
    lane = _lane(logits.shape)
    lane_f = lane.astype(F32)
    neg = -jnp.inf
    is_g = (lane >= N_EXPERTS) & (lane < N_EXPERTS + N_GROUPS)
    lg = jnp.where(is_g, logits, neg)
    eg = jnp.exp(lg - jnp.max(lg, axis=-1, keepdims=True))
    pg = eg / jnp.sum(eg, axis=-1, keepdims=True)
    g_w = jnp.max(pg, axis=-1, keepdims=True)
    g_lane = _min_lane(is_g & (pg == g_w), lane_f)
    g_idx = g_lane - float(N_EXPERTS)
    in_grp = (lane < N_EXPERTS) & ((lane // EXPERTS_PER_GROUP).astype(F32) == g_idx)
    ls = jnp.where(in_grp, logits, neg)
    es = jnp.exp(ls - jnp.max(ls, axis=-1, keepdims=True))
    ps = es / jnp.sum(es, axis=-1, keepdims=True)
    p1 = jnp.max(jnp.where(in_grp, ps, -1.0), axis=-1, keepdims=True)
    i1 = _min_lane(in_grp & (ps == p1), lane_f)
    rest = in_grp & (lane_f != i1)
    p2 = jnp.max(jnp.where(rest, ps, -1.0), axis=-1, keepdims=True)
    i2 = _min_lane(rest & (ps == p2), lane_f)
    tot = p1 + p2
    w1 = g_w * (p1 / tot)
    w2 = g_w * (p2 / tot)

    oh1 = lane_f == i1
    oh2 = lane_f == i2
    r_i = lax.broadcasted_iota(jnp.int32, (TM, TM), 0)
    c_i = lax.broadcasted_iota(jnp.int32, (TM, TM), 1)
    tri = (r_i > c_i).astype(BF16)
    before1 = jnp.dot(tri, oh1.astype(BF16), preferred_element_type=F32)
    before2 = jnp.dot(tri, oh2.astype(BF16), preferred_element_type=F32)
    tot1 = jnp.sum(oh1.astype(F32), axis=0, keepdims=True)
    tot2 = jnp.sum(oh2.astype(F32), axis=0, keepdims=True)
    base = cnt_acc[...]
    r1 = jnp.sum(jnp.where(oh1, base + before1, 0.0), axis=-1, keepdims=True)
    r2 = jnp.sum(jnp.where(oh2, base + tot1 + before2, 0.0), axis=-1, keepdims=True)
    new_cnt = base + tot1 + tot2
    cnt_acc[...] = new_cnt
    cnt_ref[...] = jnp.broadcast_to(new_cnt, cnt_ref.shape)

    vals = (i1, i2, w1, w2, r1, r2)
    route = jnp.zeros(logits.shape, F32)
    for idx, val in enumerate(vals):
        route = jnp.where(lane == idx, val, route)
    route_ref[0] = route


def _post(xcat, modsel, gb, zc, ym, yd, yg, conv_w, wout, ln_g, ln_b, wr, br, joff, nct, alpha):
    b, t, d = xcat.shape
    nt = t // TM
    nj = nt - joff
    hb = TM // SUBLANES

    def tile(width):
        return pl.BlockSpec((1, TM, width), lambda bi, j: (bi, j + joff, 0))

    def otile(width):
        return pl.BlockSpec((1, TM, width), lambda bi, j: (bi, j, 0))

    def full(arr):
        return pl.BlockSpec(arr.shape, lambda bi, j: (0,) * arr.ndim)

    kern = functools.partial(_post_kernel, joff=joff, nct=nct, nt=nt, alpha=alpha)
    return pl.pallas_call(
        kern,
        grid=(b, nj),
        in_specs=[tile(d),
                  pl.BlockSpec((1, 1, 6, d), lambda bi, j: (bi, jnp.minimum((j + joff) // nct, 1), 0, 0)),
                  tile(256), tile(256),
                  pl.BlockSpec((1, SUBLANES, 256), lambda bi, j: (bi, jnp.maximum((j + joff) * hb - 1, 0), 0)),
                  pl.BlockSpec((1, SUBLANES, 256),
                               lambda bi, j: (bi, jnp.minimum((j + joff + 1) * hb, nt * hb - 1), 0)),
                  otile(256), otile(256), otile(256),
                  full(conv_w), full(wout), full(ln_g), full(ln_b), full(wr), full(br)],
        out_specs=[otile(d), otile(d), otile(LANES), pl.BlockSpec((SUBLANES, LANES), lambda bi, j: (0, 0))],
        out_shape=[jax.ShapeDtypeStruct((b, nj * TM, d), F32),
                   jax.ShapeDtypeStruct((b, nj * TM, d), F32),
                   jax.ShapeDtypeStruct((b, nj * TM, LANES), F32),
                   jax.ShapeDtypeStruct((SUBLANES, LANES), F32)],
        scratch_shapes=[pltpu.VMEM((1, LANES), F32)],
        compiler_params=_cparams(("arbitrary", "arbitrary")),
        name="post",
    )(xcat, modsel, gb, zc, zc, zc, ym, yd, yg, conv_w, wout, ln_g, ln_b, wr, br)


def _dispatch_kernel(dest_ref, tok_ref, xs_in_ref, xs_ref, sem):
    del xs_in_ref
    base = pl.program_id(0) * (2 * TM)

    def row_copy(r, k):
        return pltpu.make_async_copy(tok_ref.at[pl.ds(r, 1)], xs_ref.at[pl.ds(dest_ref[base + 2 * r + k], 1)], sem)

    def start(r, carry):
        row_copy(r, 0).start()
        row_copy(r, 1).start()
        return carry

    def wait(r, carry):
        row_copy(r, 0).wait()
        row_copy(r, 1).wait()
        return carry

    lax.fori_loop(0, TM, start, 0)
    lax.fori_loop(0, TM, wait, 0)


def _dispatch(dest, tok2d, n_slots):
    n, d = tok2d.shape
    xs0 = jnp.zeros((n_slots, d), tok2d.dtype)
    return pl.pallas_call(
        _dispatch_kernel,
        grid_spec=pltpu.PrefetchScalarGridSpec(
            num_scalar_prefetch=1,
            grid=(n // TM,),
            in_specs=[pl.BlockSpec((TM, d), lambda i, dest: (i, 0)),
                      pl.BlockSpec(memory_space=pl.ANY)],
            out_specs=pl.BlockSpec(memory_space=pl.ANY),
            scratch_shapes=[pltpu.SemaphoreType.DMA(())]),
        out_shape=jax.ShapeDtypeStruct((n_slots, d), tok2d.dtype),
        input_output_aliases={2: 0},
        compiler_params=_cparams(("arbitrary",)),
        name="dispatch",
    )(dest, tok2d, xs0)


def _expert_kernel(be_ref, nv_ref, xs_ref, w1_ref, w3_ref, w2_ref, ys_ref):
    i = pl.program_id(0)

    @pl.when(i < nv_ref[0])
    def _():
        xb = xs_ref[...].astype(BF16)
        h1 = jnp.dot(xb, w1_ref[0].astype(BF16), preferred_element_type=F32)
        h3 = jnp.dot(xb, w3_ref[0].astype(BF16), preferred_element_type=F32)
        a = (h1 * jax.nn.sigmoid(h1) * h3).astype(BF16)
        ys_ref[...] = jnp.dot(a, w2_ref[0].astype(BF16), preferred_element_type=F32)

    @pl.when(i >= nv_ref[0])
    def _():
        ys_ref[...] = jnp.zeros_like(ys_ref)


def _experts(block_e, n_valid, xs, w1, w3, w2):
    n_slots, d = xs.shape
    de = w1.shape[2]
    return pl.pallas_call(
        _expert_kernel,
        grid_spec=pltpu.PrefetchScalarGridSpec(
            num_scalar_prefetch=2,
            grid=(n_slots // EB,),
            in_specs=[pl.BlockSpec((EB, d), lambda i, be, nv: (i, 0)),
                      pl.BlockSpec((1, d, de), lambda i, be, nv: (be[i], 0, 0)),
                      pl.BlockSpec((1, d, de), lambda i, be, nv: (be[i], 0, 0)),
                      pl.BlockSpec((1, de, d), lambda i, be, nv: (be[i], 0, 0))],
            out_specs=pl.BlockSpec((EB, d), lambda i, be, nv: (i, 0))),
        out_shape=jax.ShapeDtypeStruct((n_slots, d), F32),
        compiler_params=_cparams(("arbitrary",)),
        name="experts",
    )(block_e, n_valid, xs, w1, w3, w2)


def _combine_kernel(dest_ref, x1_ref, mod_ref, route_ref, g_ref, b_ref, ys_ref, o_ref, buf, sem, *, nj, alpha):
    base = (pl.program_id(0) * nj + pl.program_id(1)) * (2 * TM)

    def row_copy(r, k):
        return pltpu.make_async_copy(ys_ref.at[pl.ds(dest_ref[base + 2 * r + k], 1)],
                                     buf.at[pl.ds(k * TM + r, 1)], sem)

    def start(r, carry):
        row_copy(r, 0).start()
        row_copy(r, 1).start()
        return carry

    def wait(r, carry):
        row_copy(r, 0).wait()
        row_copy(r, 1).wait()
        return carry

    lax.fori_loop(0, TM, start, 0)
    lax.fori_loop(0, TM, wait, 0)

    route = route_ref[0]
    f = route[:, 2:3] * buf[0:TM, :] + route[:, 3:4] * buf[TM:2 * TM, :]
    g2 = mod_ref[0, 0, 5:6, :]
    o_ref[0] = _layer_norm(alpha * x1_ref[0] + g2 * f, g_ref[...], b_ref[...])


def _combine(dest, x1, modsel, route, ln_g, ln_b, ys, joff, nct, alpha):
    b, n, d = x1.shape
    nj = n // TM

    def full(arr):
        return pl.BlockSpec(arr.shape, lambda bi, j, dest: (0,) * arr.ndim)

    kern = functools.partial(_combine_kernel, nj=nj, alpha=alpha)
    return pl.pallas_call(
        kern,
        grid_spec=pltpu.PrefetchScalarGridSpec(
            num_scalar_prefetch=1,
            grid=(b, nj),
            in_specs=[pl.BlockSpec((1, TM, d), lambda bi, j, dest: (bi, j, 0)),
                      pl.BlockSpec((1, 1, 6, d),
                                   lambda bi, j, dest: (bi, jnp.minimum((j + joff) // nct, 1), 0, 0)),
                      pl.BlockSpec((1, TM, LANES), lambda bi, j, dest: (bi, j, 0)),
                      full(ln_g), full(ln_b),
                      pl.BlockSpec(memory_space=pl.ANY)],
            out_specs=pl.BlockSpec((1, TM, d), lambda bi, j, dest: (bi, j, 0)),
            scratch_shapes=[pltpu.VMEM((2 * TM, d), F32), pltpu.SemaphoreType.DMA(())]),
        out_shape=jax.ShapeDtypeStruct((b, n, d), F32),
        compiler_params=_cparams(("arbitrary", "arbitrary")),
        name="combine",
    )(dest, x1, modsel, route, ln_g, ln_b, ys)


def _rope_tables(seq, ctx_len):
    t = jnp.arange(seq, dtype=jnp.int32)
    row = (t // GRID_W).astype(F32)
    col = (t % GRID_W).astype(F32)

    def table(vec_dim):
        half = vec_dim // 4
        freqs = ROPE_THETA ** (-jnp.arange(half, dtype=F32) / half)
        lane = np.arange(vec_dim)
        idx = lane % half
        use_col = (lane // (2 * half)) == 1
        ang = jnp.where(use_col[None, :], col[:, None], row[:, None]) * freqs[idx][None, :]
        sign = np.where((lane % (2 * half)) < half, -1.0, 1.0).astype(np.float32)
        return jnp.cos(ang), jnp.sin(ang) * sign[None, :]

    def with_ctx(cos, sin):
        width = cos.shape[1]
        return (jnp.concatenate([jnp.ones((ctx_len, width), F32), cos], axis=0),
                jnp.concatenate([jnp.zeros((ctx_len, width), F32), sin], axis=0))

    c32, s32 = table(32)
    c64, s64 = table(64)
    cd, sd = with_ctx(jnp.tile(c32, (1, 4)), jnp.tile(s32, (1, 4)))
    cg, sg = with_ctx(jnp.tile(c64, (1, 2)), jnp.tile(s64, (1, 2)))
    ones, zeros = jnp.ones((seq, 64), F32), jnp.zeros((seq, 64), F32)
    cm, sm = with_ctx(jnp.concatenate([ones, c32, ones[:, :32]], axis=1),
                      jnp.concatenate([zeros, s32, zeros[:, :32]], axis=1))
    return cm, sm, cd, sd, cg, sg


def _in_proj_columns():
    def spread(start, odd_blocks):
        cols = []
        for i, odd in enumerate(odd_blocks):
            vals = list(range(start + 64 * i, start + 64 * (i + 1)))
            cols += ([-1] * 64 + vals) if odd else (vals + [-1] * 64)
        return cols

    cols = list(range(0, IN_CONV))
    cols += list(range(OFF_MLA, OFF_MLA + MLA_Q_LORA + MLA_KV_LORA))
    cols += [-1] * 64 + list(range(OFF_MLA + MLA_Q_LORA + MLA_KV_LORA, OFF_DIFF)) + [-1] * 32
    cols += list(range(OFF_DIFF, OFF_DIFF + 2 * DIFF_QW))
    cols += spread(OFF_DIFF + 2 * DIFF_QW, [h % 2 for h in range(DIFF_HEADS)])
    q0 = OFF_GQA
    head = lambda h: list(range(q0 + GQA_HD * h, q0 + GQA_HD * (h + 1)))
    cols += head(0) + head(2) + head(1) + head(3)
    cols += list(range(OFF_GQA + GQA_QW, OFF_GQA + GQA_QW + GQA_KW))
    v0 = OFF_GQA + GQA_QW + GQA_KW
    for kvh in range(GQA_KV_HEADS):
        vals = list(range(v0 + GQA_HD * kvh, v0 + GQA_HD * (kvh + 1)))
        cols += vals + [-1] * 64 + [-1] * 64 + vals
    assert len(cols) == P_END
    return np.asarray(cols, np.int32)


def _relayout_columns(w, cols):
    valid = jnp.asarray(cols >= 0)
    return jnp.where(valid[None, :], jnp.take(w, jnp.asarray(np.maximum(cols, 0)), axis=1), 0.0)


def _mla_weights(w_uq, w_qr, w_uk, w_uv):
    zq = jnp.zeros((MLA_Q_LORA, 32), F32)
    zk = jnp.zeros((MLA_KV_LORA, 64), F32)
    wq = jnp.concatenate([blk for h in range(MLA_HEADS)
                          for blk in (w_uq[:, 64 * h:64 * (h + 1)], w_qr[:, 32 * h:32 * (h + 1)], zq)], axis=1)
    wk = jnp.concatenate([blk for h in range(MLA_HEADS) for blk in (w_uk[:, 64 * h:64 * (h + 1)], zk)], axis=1)
    wv = jnp.concatenate([blk for h in range(MLA_HEADS)
                          for blk in ((zk, w_uv[:, 64 * h:64 * (h + 1)]) if h % 2 else
                                      (w_uv[:, 64 * h:64 * (h + 1)], zk))], axis=1)
    return wq.astype(BF16), jnp.concatenate([wk, wv], axis=1).astype(BF16)


def _slot_plan(route, counts, n_slots):
    cnt = counts.astype(jnp.int32)
    padded = (cnt + EB - 1) // EB * EB
    pad_ends = jnp.cumsum(padded)
    pad_starts = pad_ends - padded
    e = route[:, 0:2].astype(jnp.int32)
    rank = route[:, 4:6].astype(jnp.int32)
    dest = (pad_starts[e] + rank).reshape(-1)
    nblk = n_slots // EB
    block_start = jnp.arange(nblk, dtype=jnp.int32) * EB
    block_e = jnp.minimum(jnp.sum((pad_ends[None, :] <= block_start[:, None]).astype(jnp.int32), axis=1),
                          N_EXPERTS - 1)
    n_valid = (pad_ends[-1:] // EB).astype(jnp.int32)
    return dest, block_e, n_valid


def kernel(x, c, ctx, c_ctx, w_ada, b_ada, w_in, w_out, conv_w, mla_q_norm, mla_kv_norm, mla_w_uq, mla_w_qr,
           mla_w_uk, mla_w_uv, diff_lambda, diff_subln, gqa_q_norm, gqa_k_norm, ln1_g, ln1_b, ln2_g, ln2_b,
           moe_w_group, moe_b_group, moe_w_sub, moe_b_sub, moe_w1, moe_w3, moe_w2):
    b, s, d = x.shape
    n_ctx = ctx.shape[1]
    depth = w_in.shape[0]
    assert n_ctx % TM == 0 and s % TM == 0 and s % GRID_W == 0
    nct = n_ctx // TM
    t = n_ctx + s
    alpha = (2 * depth) ** 0.25

    rows = ((b + 1 + SUBLANES - 1) // SUBLANES) * SUBLANES
    cc = jnp.zeros((rows, d), F32).at[:b].set(c).at[b].set(c_ctx)
    mod = _modulation(cc, w_ada, b_ada)
    tabs = _rope_tables(s, n_ctx)
    cols = _in_proj_columns()
    xcat = jnp.concatenate([ctx, x], axis=1)

    for l in range(depth):
        last = l == depth - 1
        joff = nct if last else 0
        lam_init = 0.8 - 0.6 * math.exp(-0.3 * l)
        ml = mod[l].reshape(rows, 6, d)
        modsel = jnp.stack([jnp.broadcast_to(ml[b], (b, 6, d)), ml[:b]], axis=1)
        win_p = _relayout_columns(w_in[l], cols).astype(BF16)
        wq, wkv = _mla_weights(mla_w_uq[l], mla_w_qr[l], mla_w_uk[l], mla_w_uv[l])
        gq = jnp.tile(gqa_q_norm[l], 2)[None, :]
        gk = jnp.tile(gqa_k_norm[l], 2)[None, :]
        gb, zc, qm, km, vm, qd, kd, vd, qg, kg, vg = _proj(
            xcat, modsel, win_p, tabs, mla_q_norm[l][None, :], mla_kv_norm[l][None, :], wq, wkv, gq, gk, nct)

        ym = _attention(qm, km, vm, (), _mla_maps(), _mla_finish, joff, nct, "attn_mla")
        yd = _attention(qd, kd, vd, (diff_lambda[l], jnp.tile(diff_subln[l], 2)[None, :]), _diff_maps(),
                        functools.partial(_diff_finish, lam_init), joff, nct, "attn_diff")
        yg = _attention(qg, kg, vg, (), _gqa_maps(), _gqa_finish, joff, nct, "attn_gqa")

        wr = jnp.zeros((d, LANES), F32).at[:, :N_EXPERTS].set(moe_w_sub[l])
        wr = wr.at[:, N_EXPERTS:N_EXPERTS + N_GROUPS].set(moe_w_group[l])
        br = jnp.zeros((1, LANES), F32).at[0, :N_EXPERTS].set(moe_b_sub[l])
        br = br.at[0, N_EXPERTS:N_EXPERTS + N_GROUPS].set(moe_b_group[l])
        x1, tok, route, counts = _post(xcat, modsel, gb, zc, ym, yd, yg, conv_w[l], w_out[l].astype(BF16),
                                       ln1_g[l][None, :], ln1_b[l][None, :], wr, br, joff, nct, alpha)

        n_tok = tok.shape[0] * tok.shape[1]
        n_slots = 2 * n_tok + N_EXPERTS * EB
        dest, block_e, n_valid = _slot_plan(route.reshape(n_tok, LANES), counts[0, :N_EXPERTS], n_slots)
        xs = _dispatch(dest, tok.reshape(n_tok, d), n_slots)
        ys = _experts(block_e, n_valid, xs, moe_w1[l], moe_w3[l], moe_w2[l])
        xcat = _combine(dest, x1, modsel, route, ln2_g[l][None, :], ln2_b[l][None, :], ys, joff, nct, alpha)
    return xcat
```

```python
import functools
import math

import jax
import jax.numpy as jnp
import numpy as np
from jax import lax
from jax.experimental import pallas as pl
from jax.experimental.pallas import tpu as pltpu

F32 = jnp.float32
BF16 = jnp.bfloat16

GRID_W = 64
ROPE_THETA = 10000.0
NORM_EPS = 1e-6
CONV_CH = 256
MLA_HEADS, MLA_Q_LORA, MLA_KV_LORA, MLA_NOPE, MLA_ROPE, MLA_V = 4, 256, 128, 64, 32, 64
DIFF_HEADS, DIFF_QK, DIFF_V = 4, 32, 64
GQA_Q_HEADS, GQA_KV_HEADS, GQA_HD = 4, 2, 64
IN_CONV = 3 * CONV_CH
IN_MLA = MLA_Q_LORA + MLA_KV_LORA + MLA_ROPE
DIFF_QW = DIFF_HEADS * 2 * DIFF_QK
IN_DIFF = 2 * DIFF_QW + DIFF_HEADS * DIFF_V
GQA_QW = GQA_Q_HEADS * GQA_HD
GQA_KW = GQA_KV_HEADS * GQA_HD
OFF_MLA = IN_CONV
OFF_DIFF = OFF_MLA + IN_MLA
OFF_GQA = OFF_DIFF + IN_DIFF
MLA_SCALE = (MLA_NOPE + MLA_ROPE) ** -0.5
DIFF_SCALE = DIFF_QK ** -0.5
GQA_SCALE = GQA_HD ** -0.5
N_GROUPS, EXPERTS_PER_GROUP = 4, 8
N_EXPERTS = N_GROUPS * EXPERTS_PER_GROUP
D_EXPERT = 256
LOG2E = math.log2(math.e)

LANES = 128
SUBLANES = 8
TM = 256
EB = 256
VMEM_LIMIT = 56 * 1024 * 1024

P_CONV = 0
P_MLA = P_CONV + 768
P_DIFF = P_MLA + 512
P_GQA = P_DIFF + 1024
P_END = P_GQA + 896


def _cparams(sem):
    return pltpu.CompilerParams(dimension_semantics=sem, vmem_limit_bytes=VMEM_LIMIT)


def _lane(shape):
    return lax.broadcasted_iota(jnp.int32, shape, len(shape) - 1)


def _mod_kernel(c_ref, w_ref, b_ref, o_ref):
    c = c_ref[...]
    sc = (c * jax.nn.sigmoid(c)).astype(BF16)
    o_ref[0] = jnp.dot(sc, w_ref[0].astype(BF16), preferred_element_type=F32) + b_ref[0]


def _modulation(cc, w_ada, b_ada):
    n_layers, d, d6 = w_ada.shape
    r = cc.shape[0]
    return pl.pallas_call(
        _mod_kernel,
        grid=(n_layers, d6 // d),
        in_specs=[pl.BlockSpec((r, d), lambda l, n: (0, 0)),
                  pl.BlockSpec((1, d, d), lambda l, n: (l, 0, n)),
                  pl.BlockSpec((1, 1, d), lambda l, n: (l, 0, n))],
        out_specs=pl.BlockSpec((1, r, d), lambda l, n: (l, 0, n)),
        out_shape=jax.ShapeDtypeStruct((n_layers, r, d6), F32),
        compiler_params=_cparams(("arbitrary", "arbitrary")),
        name="mod",
    )(cc, w_ada, b_ada.reshape(n_layers, 1, d6))


def _rope(z, cos, sin_signed, half):
    n = z.shape[-1]
    first = (_lane(z.shape) % (2 * half)) < half
    partner = jnp.where(first, pltpu.roll(z, n - half, 1), pltpu.roll(z, half, 1))
    return z * cos + partner * sin_signed


def _rms(z, gain):
    return z * lax.rsqrt(jnp.mean(z * z, axis=-1, keepdims=True) + NORM_EPS) * gain


def _seg_rms64(z, gain):
    lo = _lane(z.shape) < 64
    z2 = z * z
    s_lo = jnp.sum(jnp.where(lo, z2, 0.0), axis=-1, keepdims=True)
    s_hi = jnp.sum(jnp.where(lo, 0.0, z2), axis=-1, keepdims=True)
    ms = jnp.where(lo, s_lo, s_hi) * (1.0 / 64)
    return z * lax.rsqrt(ms + NORM_EPS) * gain


def _ones_lane(odd):
    return (_lane((1, LANES)) == (0 if odd else 64)).astype(F32)


def _proj_kernel(x_ref, mod_ref, win_ref, cm_ref, sm_ref, cd_ref, sd_ref, cg_ref, sg_ref,
                 qn_ref, kvn_ref, wq_ref, wkv_ref, gq_ref, gk_ref,
                 gb_ref, zc_ref, qm_ref, km_ref, vm_ref, qd_ref, kd_ref, vd_ref, qg_ref, kg_ref, vg_ref):
    x = x_ref[0]
    sh1 = mod_ref[0, 0, 0:1, :]
    sc1 = mod_ref[0, 0, 1:2, :]
    xm = (x * (1.0 + sc1) + sh1).astype(BF16)

    uc = jnp.dot(xm, win_ref[:, P_CONV:P_MLA], preferred_element_type=F32)
    gb_ref[0] = uc[:, 0:256]
    zc_ref[0] = uc[:, 256:512] * uc[:, 512:768]

    um = jnp.dot(xm, win_ref[:, P_MLA:P_DIFF], preferred_element_type=F32)
    cm, sm = cm_ref[...], sm_ref[...]
    cq = _rms(um[:, 0:256], qn_ref[...]).astype(BF16)
    qall = jnp.dot(cq, wq_ref[...], preferred_element_type=F32)
    ckv = _rms(um[:, 256:384], kvn_ref[...]).astype(BF16)
    kv = jnp.dot(ckv, wkv_ref[...], preferred_element_type=F32)
    kr = _rope(um[:, 384:512], cm, sm, 8)
    for h in range(MLA_HEADS):
        sl = slice(LANES * h, LANES * (h + 1))
        qm_ref[0, :, sl] = (_rope(qall[:, sl], cm, sm, 8) * (MLA_SCALE * LOG2E)).astype(BF16)
        km_ref[0, :, sl] = (kv[:, sl] + kr).astype(BF16)
        vm_ref[0, :, sl] = (kv[:, 512 + LANES * h:512 + LANES * (h + 1)] + _ones_lane(h % 2)).astype(BF16)

    ud = jnp.dot(xm, win_ref[:, P_DIFF:P_GQA], preferred_element_type=F32)
    cd, sd = cd_ref[...], sd_ref[...]
    for c in range(2):
        sl = slice(LANES * c, LANES * (c + 1))
        qd_ref[0, :, sl] = (_rope(ud[:, sl], cd, sd, 8) * (DIFF_SCALE * LOG2E)).astype(BF16)
        kd_ref[0, :, sl] = _rope(ud[:, 256 + LANES * c:256 + LANES * (c + 1)], cd, sd, 8).astype(BF16)
    for h in range(DIFF_HEADS):
        sl = slice(LANES * h, LANES * (h + 1))
        vd_ref[0, :, sl] = (ud[:, 512 + LANES * h:512 + LANES * (h + 1)] + _ones_lane(h % 2)).astype(BF16)

    ug = jnp.dot(xm, win_ref[:, P_GQA:P_END], preferred_element_type=F32)
    cg, sg = cg_ref[...], sg_ref[...]
    for c in range(2):
        sl = slice(LANES * c, LANES * (c + 1))
        qg_ref[0, :, sl] = (_rope(_seg_rms64(ug[:, sl], gq_ref[...]), cg, sg, 16) * (GQA_SCALE * LOG2E)).astype(BF16)
    kg_ref[0] = _rope(_seg_rms64(ug[:, 256:384], gk_ref[...]), cg, sg, 16).astype(BF16)
    for i in range(4):
        sl = slice(LANES * i, LANES * (i + 1))
        vg_ref[0, :, sl] = (ug[:, 384 + LANES * i:384 + LANES * (i + 1)] + _ones_lane(i % 2)).astype(BF16)


def _proj(xcat, modsel, win_p, tabs, qn, kvn, wq, wkv, gq, gk, nct):
    b, t, d = xcat.shape
    nt = t // TM

    def tile(width):
        return pl.BlockSpec((1, TM, width), lambda bi, j: (bi, j, 0))

    def full(arr):
        return pl.BlockSpec(arr.shape, lambda bi, j: (0,) * arr.ndim)

    tab_spec = pl.BlockSpec((TM, LANES), lambda bi, j: (j, 0))
    widths = [(256, F32), (256, F32), (512, BF16), (512, BF16), (512, BF16), (256, BF16), (256, BF16),
              (512, BF16), (256, BF16), (128, BF16), (512, BF16)]
    return pl.pallas_call(
        _proj_kernel,
        grid=(b, nt),
        in_specs=[tile(d),
                  pl.BlockSpec((1, 1, 6, d), lambda bi, j: (bi, jnp.minimum(j // nct, 1), 0, 0)),
                  full(win_p)] + [tab_spec] * 6 + [full(a) for a in (qn, kvn, wq, wkv, gq, gk)],
        out_specs=[tile(w) for w, _ in widths],
        out_shape=[jax.ShapeDtypeStruct((b, t, w), dt) for w, dt in widths],
        compiler_params=_cparams(("arbitrary", "arbitrary")),
        name="proj",
    )(xcat, modsel, win_p, *tabs, qn, kvn, wq, wkv, gq, gk)


def _scores(q_ref, k_ref, tk, chunk, mask):
    q = q_ref[0, :, LANES * chunk:LANES * (chunk + 1)]
    if mask is not None:
        lane = _lane(q.shape)
        q = jnp.where((lane >= mask[0]) & (lane < mask[1]), q, jnp.zeros_like(q))
    k = k_ref[0, 0:tk, LANES * chunk:LANES * (chunk + 1)]
    return lax.dot_general(q, k, (((1,), (1,)), ((), ())), preferred_element_type=F32)


def _attend(s, v, odd):
    p = jnp.exp2(s - jnp.max(s, axis=-1, keepdims=True)).astype(BF16)
    o = jnp.dot(p, v, preferred_element_type=F32)
    den = o[:, 0:1] if odd else o[:, 64:65]
    return o * (1.0 / den)


def _pair(even, odd):
    return jnp.where(_lane(even.shape) < 64, even, odd)


def _vblock(v_ref, tk, idx):
    return v_ref[0, 0:tk, LANES * idx:LANES * (idx + 1)]


def _one_ahead(n, score_fn, finish_fn):
    outs, nxt = [], score_fn(0)
    for h in range(n):
        cur = nxt
        if h + 1 < n:
            nxt = score_fn(h + 1)
        outs.append(finish_fn(h, cur))
    return outs


def _mla_heads(q_ref, k_ref, v_ref, tk, extra):
    o = _one_ahead(MLA_HEADS, lambda h: _scores(q_ref, k_ref, tk, h, None),
                   lambda h, s: _attend(s, _vblock(v_ref, tk, h), h % 2))
    return jnp.concatenate([_pair(o[0], o[1]), _pair(o[2], o[3])], axis=-1)


def _gqa_heads(q_ref, k_ref, v_ref, tk, extra):
    def scores(i):
        kvh, g = divmod(i, GQA_Q_HEADS // GQA_KV_HEADS)
        q = q_ref[0, :, LANES * g:LANES * (g + 1)]
        lane = _lane(q.shape)
        q = jnp.where((lane >= 64 * kvh) & (lane < 64 * kvh + 64), q, jnp.zeros_like(q))
        return lax.dot_general(q, k_ref[0, 0:tk, :], (((1,), (1,)), ((), ())), preferred_element_type=F32)

    o = _one_ahead(GQA_Q_HEADS, scores, lambda i, s: _attend(s, _vblock(v_ref, tk, i), i % 2))
    return jnp.concatenate([_pair(o[0], o[1]), _pair(o[2], o[3])], axis=-1)


def _diff_heads(lam_init, q_ref, k_ref, v_ref, tk, extra):
    dl_ref, subln_ref = extra
    dl = dl_ref[...]
    lam = (jnp.exp(jnp.sum(dl[0:1] * dl[1:2], axis=-1, keepdims=True))
           - jnp.exp(jnp.sum(dl[2:3] * dl[3:4], axis=-1, keepdims=True)) + lam_init)

    def head_scores(h):
        return [_scores(q_ref, k_ref, tk, (2 * h + m) // 4, (32 * ((2 * h + m) % 4), 32 * ((2 * h + m) % 4) + 32))
                for m in range(2)]

    def finish(h, pair):
        probs = []
        for s in pair:
            e = jnp.exp2(s - jnp.max(s, axis=-1, keepdims=True))
            probs.append((e, jnp.sum(e, axis=-1, keepdims=True)))
        a = probs[0][0] * (1.0 / probs[0][1]) - probs[1][0] * (lam / probs[1][1])
        return jnp.dot(a.astype(BF16), _vblock(v_ref, tk, h), preferred_element_type=F32)

    d = _one_ahead(DIFF_HEADS, head_scores, finish)
    chunks = [_seg_rms64(_pair(d[2 * c], d[2 * c + 1]), subln_ref[...]) * (1.0 - lam_init) for c in range(2)]
    return jnp.concatenate(chunks, axis=-1)


def _attn_kernel(*refs, heads, joff, nct, ctx_len, total_len):
    q_ref, k_ref, v_ref = refs[:3]
    extra, o_ref = refs[3:-1], refs[-1]
    jj = pl.program_id(1) + joff

    def run(tk):
        o_ref[0] = heads(q_ref, k_ref, v_ref, tk, extra).astype(o_ref.dtype)

    if joff < nct:
        pl.when(jj < nct)(lambda: run(ctx_len))
        pl.when(jj >= nct)(lambda: run(total_len))
    else:
        run(total_len)


def _attention(q, k, v, extra, heads, joff, nct, name):
    b, t, _ = q.shape
    nt = t // TM

    def full(arr):
        return pl.BlockSpec(arr.shape, lambda bi, j: (0,) * arr.ndim)

    kern = functools.partial(_attn_kernel, heads=heads, joff=joff, nct=nct, ctx_len=nct * TM, total_len=t)
    return pl.pallas_call(
        kern,
        grid=(b, nt - joff),
        in_specs=[pl.BlockSpec((1, TM, q.shape[2]), lambda bi, j: (bi, j + joff, 0)),
                  pl.BlockSpec((1, t, k.shape[2]), lambda bi, j: (bi, 0, 0)),
                  pl.BlockSpec((1, t, v.shape[2]), lambda bi, j: (bi, 0, 0))] + [full(a) for a in extra],
        out_specs=pl.BlockSpec((1, TM, 256), lambda bi, j: (bi, j, 0)),
        out_shape=jax.ShapeDtypeStruct((b, t - joff * TM, 256), BF16),
        compiler_params=_cparams(("arbitrary", "arbitrary")),
        name=name,
    )(q, k, v, *extra)


def _layer_norm(h, g, bias):
    mu = jnp.mean(h, axis=-1, keepdims=True)
    hc = h - mu
    var = jnp.mean(hc * hc, axis=-1, keepdims=True)
    return hc * lax.rsqrt(var + NORM_EPS) * g + bias


def _store_token_tiles(ref, val):
    rows = val.shape[0]
    for c in range(SUBLANES):
        ref[pl.ds(c, rows, stride=SUBLANES), :] = val[:, LANES * c:LANES * (c + 1)]


def _load_token_tiles(ref, rows):
    return jnp.concatenate([ref[pl.ds(c, rows, stride=SUBLANES), :] for c in range(SUBLANES)], axis=1)


def _min_lane(cond, lane_f):
    return jnp.min(jnp.where(cond, lane_f, float(LANES)), axis=-1, keepdims=True)


def _post_kernel(x_ref, mod_ref, gb_ref, zc_ref, zp_ref, zn_ref, ym_ref, yd_ref, yg_ref, cw_ref, wout_ref,
                 g_ref, b_ref, wr_ref, br_ref, x1_ref, tok_ref, route_ref, cnt_ref, cnt_acc,
                 *, joff, nct, nt, alpha):
    jj = pl.program_id(1) + joff
    first_step = (pl.program_id(0) == 0) & (pl.program_id(1) == 0)

    @pl.when(first_step)
    def _():
        cnt_acc[...] = jnp.zeros_like(cnt_acc)

    zc = zc_ref[0]
    row = lax.broadcasted_iota(jnp.int32, zc.shape, 0)
    left_ok = (jj != 0) & (jj != nct)
    right_ok = (jj != nct - 1) & (jj != nt - 1)
    halo_prev = jnp.where(left_ok, zp_ref[0, SUBLANES - 1:SUBLANES, :], 0.0)
    halo_next = jnp.where(right_ok, zn_ref[0, 0:1, :], 0.0)
    zprev = jnp.where(row == 0, halo_prev, pltpu.roll(zc, 1, 0))
    znext = jnp.where(row == TM - 1, halo_next, pltpu.roll(zc, TM - 1, 0))
    cw = cw_ref[...]
    conv = zprev * cw[0:1] + zc * cw[1:2] + znext * cw[2:3]
    yc = (gb_ref[0] * conv).astype(BF16)

    y = jnp.concatenate([yc, ym_ref[0], yd_ref[0], yg_ref[0]], axis=-1)
    acc = jnp.dot(y, wout_ref[...], preferred_element_type=F32)

    g1 = mod_ref[0, 0, 2:3, :]
    sh2 = mod_ref[0, 0, 3:4, :]
    sc2 = mod_ref[0, 0, 4:5, :]
    x1 = _layer_norm(alpha * x_ref[0] + g1 * acc, g_ref[...], b_ref[...])
    x1_ref[0] = x1
    tok = x1 * (1.0 + sc2) + sh2
    _store_token_tiles(tok_ref, tok)

    tok_hi = tok.astype(BF16)
    tok_lo = (tok - tok_hi.astype(F32)).astype(BF16)
    part = jnp.dot(tok_hi, wr_ref[...], preferred_element_type=F32)
    logits = (part[:, 0:LANES] + (part[:, LANES:2 * LANES]
                                  + jnp.dot(tok_lo, wr_ref[:, 0:LANES], preferred_element_type=F32))) + br_ref[...]
    lane = _lane(logits.shape)
    lane_f = lane.astype(F32)
    neg = -jnp.inf
    is_g = (lane >= N_EXPERTS) & (lane < N_EXPERTS + N_GROUPS)
    lg = jnp.where(is_g, logits, neg)
    eg = jnp.exp(lg - jnp.max(lg, axis=-1, keepdims=True))
    pg = eg / jnp.sum(eg, axis=-1, keepdims=True)
    g_w = jnp.max(pg, axis=-1, keepdims=True)
    g_lane = _min_lane(is_g & (pg == g_w), lane_f)
    g_idx = g_lane - float(N_EXPERTS)
    in_grp = (lane < N_EXPERTS) & ((lane // EXPERTS_PER_GROUP).astype(F32) == g_idx)
    ls = jnp.where(in_grp, logits, neg)
    es = jnp.exp(ls - jnp.max(ls, axis=-1, keepdims=True))
    ps = es / jnp.sum(es, axis=-1, keepdims=True)
    p1 = jnp.max(jnp.where(in_grp, ps, -1.0), axis=-1, keepdims=True)
    i1 = _min_lane(in_grp & (ps == p1), lane_f)
    rest = in_grp & (lane_f != i1)
    p2 = jnp.max(jnp.where(rest, ps, -1.0), axis=-1, keepdims=True)
    i2 = _min_lane(rest & (ps == p2), lane_f)
    tot = p1 + p2
    w1 = g_w * (p1 / tot)
    w2 = g_w * (p2 / tot)

    oh1 = lane_f == i1
    oh2 = lane_f == i2
    r_i = lax.broadcasted_iota(jnp.int32, (TM, TM), 0)
    c_i = lax.broadcasted_iota(jnp.int32, (TM, TM), 1)
    tri = (r_i > c_i).astype(BF16)
    before1 = jnp.dot(tri, oh1.astype(BF16), preferred_element_type=F32)
    before2 = jnp.dot(tri, oh2.astype(BF16), preferred_element_type=F32)
    tot1 = jnp.sum(oh1.astype(F32), axis=0, keepdims=True)
    tot2 = jnp.sum(oh2.astype(F32), axis=0, keepdims=True)
    base = cnt_acc[...]
    r1 = jnp.sum(jnp.where(oh1, base + before1, 0.0), axis=-1, keepdims=True)
    r2 = jnp.sum(jnp.where(oh2, base + tot1 + before2, 0.0), axis=-1, keepdims=True)
    new_cnt = base + tot1 + tot2
    cnt_acc[...] = new_cnt
    cnt_ref[...] = jnp.broadcast_to(new_cnt, cnt_ref.shape)

    vals = (i1, i2, w1, w2, r1, r2)
    route = jnp.zeros(logits.shape, F32)
    for idx, val in enumerate(vals):
        route = jnp.where(lane == idx, val, route)
    route_ref[0] = route


def _post(xcat, modsel, gb, zc, ym, yd, yg, conv_w, wout, ln_g, ln_b, wr, br, joff, nct, alpha):
    b, t, d = xcat.shape
    nt = t // TM
    nj = nt - joff
    hb = TM // SUBLANES

    def tile(width):
        return pl.BlockSpec((1, TM, width), lambda bi, j: (bi, j + joff, 0))

    def otile(width):
        return pl.BlockSpec((1, TM, width), lambda bi, j: (bi, j, 0))

    def full(arr):
        return pl.BlockSpec(arr.shape, lambda bi, j: (0,) * arr.ndim)

    kern = functools.partial(_post_kernel, joff=joff, nct=nct, nt=nt, alpha=alpha)
    return pl.pallas_call(
        kern,
        grid=(b, nj),
        in_specs=[tile(d),
                  pl.BlockSpec((1, 1, 6, d), lambda bi, j: (bi, jnp.minimum((j + joff) // nct, 1), 0, 0)),
                  tile(256), tile(256),
                  pl.BlockSpec((1, SUBLANES, 256), lambda bi, j: (bi, jnp.maximum((j + joff) * hb - 1, 0), 0)),
                  pl.BlockSpec((1, SUBLANES, 256),
                               lambda bi, j: (bi, jnp.minimum((j + joff + 1) * hb, nt * hb - 1), 0)),
                  otile(256), otile(256), otile(256),
                  full(conv_w), full(wout), full(ln_g), full(ln_b), full(wr), full(br)],
        out_specs=[otile(d), pl.BlockSpec((TM * SUBLANES, LANES), lambda bi, j: (bi * nj + j, 0)),
                   otile(LANES), pl.BlockSpec((SUBLANES, LANES), lambda bi, j: (0, 0))],
        out_shape=[jax.ShapeDtypeStruct((b, nj * TM, d), F32),
                   jax.ShapeDtypeStruct((b * nj * TM * SUBLANES, LANES), F32),
                   jax.ShapeDtypeStruct((b, nj * TM, LANES), F32),
                   jax.ShapeDtypeStruct((SUBLANES, LANES), F32)],
        scratch_shapes=[pltpu.VMEM((1, LANES), F32)],
        compiler_params=_cparams(("arbitrary", "arbitrary")),
        name="post",
    )(xcat, modsel, gb, zc, zc, zc, ym, yd, yg, conv_w, wout, ln_g, ln_b, wr, br)


ROW_UNROLL = 16
TILE_ROWS = SUBLANES


def _expert_kernel(be_ref, info_ref, tok_ref, w1_ref, w3_ref, w2_ref, out_ref, xbuf, ybuf, gsem, ssem,
                   *, n_tok, nblk):
    del be_ref
    i = pl.program_id(0)
    slot = i % 2
    blk_rows = EB * TILE_ROWS

    def tile_at(ref, row):
        return ref.at[pl.ds(pl.multiple_of(row * TILE_ROWS, TILE_ROWS), TILE_ROWS)]

    def gather_start(blk, sl):
        def body(g, carry):
            for u in range(ROW_UNROLL):
                r = g * ROW_UNROLL + u
                src = info_ref[blk * EB + r] & 0xFFFF
                pltpu.make_async_copy(tile_at(tok_ref, src), tile_at(xbuf.at[sl], r),
                                      gsem.at[sl]).start(priority=u % 2)
            return carry
        lax.fori_loop(0, EB // ROW_UNROLL, body, 0)

    def scatter_start(blk, sl):
        def body(g, carry):
            for u in range(ROW_UNROLL):
                r = g * ROW_UNROLL + u
                info = info_ref[blk * EB + r]
                dst = (info & 0xFFFF) + (info >> 16) * n_tok
                pltpu.make_async_copy(tile_at(ybuf.at[sl], r), tile_at(out_ref, dst),
                                      ssem.at[sl]).start(priority=u % 2)
            return carry
        lax.fori_loop(0, EB // ROW_UNROLL, body, 0)

    def gather_wait(sl):
        pltpu.make_async_copy(tok_ref.at[pl.ds(0, blk_rows)], xbuf.at[sl], gsem.at[sl]).wait()

    def scatter_wait(sl):
        pltpu.make_async_copy(ybuf.at[sl], out_ref.at[pl.ds(0, blk_rows)], ssem.at[sl]).wait()

    @pl.when(i == 0)
    def _():
        gather_start(0, 0)

    @pl.when(i + 1 < nblk)
    def _():
        gather_start(i + 1, 1 - slot)

    gather_wait(slot)

    @pl.when(i >= 2)
    def _():
        scatter_wait(slot)

    xb = _load_token_tiles(xbuf.at[slot], EB).astype(BF16)
    h1 = jnp.dot(xb, w1_ref[0].astype(BF16), preferred_element_type=F32)
    h3 = jnp.dot(xb, w3_ref[0].astype(BF16), preferred_element_type=F32)
    a = (h1 * jax.nn.sigmoid(h1) * h3).astype(BF16)
    _store_token_tiles(ybuf.at[slot], jnp.dot(a, w2_ref[0].astype(BF16), preferred_element_type=F32))
    scatter_start(i, slot)

    @pl.when(i == nblk - 1)
    def _():
        scatter_wait(1 - slot)
        scatter_wait(slot)
        ybuf[0] = jnp.zeros(ybuf.shape[1:], F32)
        for half in range(2):
            cp = pltpu.make_async_copy(
                ybuf.at[0], out_ref.at[pl.ds((2 * n_tok + half * EB) * TILE_ROWS, blk_rows)], ssem.at[0])
            cp.start()
            cp.wait()


def _experts(block_e, info, tok_tiles, w1, w3, w2):
    n_tok = tok_tiles.shape[0] // TILE_ROWS
    d, de = w1.shape[1], w1.shape[2]
    nblk = info.shape[0] // EB
    assert nblk >= 2 and n_tok < (1 << 16) and d == TILE_ROWS * LANES
    kern = functools.partial(_expert_kernel, n_tok=n_tok, nblk=nblk)
    return pl.pallas_call(
        kern,
        grid_spec=pltpu.PrefetchScalarGridSpec(
            num_scalar_prefetch=2,
            grid=(nblk,),
            in_specs=[pl.BlockSpec(memory_space=pl.ANY),
                      pl.BlockSpec((1, d, de), lambda i, be, info: (be[i], 0, 0)),
                      pl.BlockSpec((1, d, de), lambda i, be, info: (be[i], 0, 0)),
                      pl.BlockSpec((1, de, d), lambda i, be, info: (be[i], 0, 0))],
            out_specs=pl.BlockSpec(memory_space=pl.ANY),
            scratch_shapes=[pltpu.VMEM((2, EB * TILE_ROWS, LANES), F32), pltpu.VMEM((2, EB * TILE_ROWS, LANES), F32),
                            pltpu.SemaphoreType.DMA((2,)), pltpu.SemaphoreType.DMA((2,))]),
        out_shape=jax.ShapeDtypeStruct(((2 * n_tok + 2 * EB) * TILE_ROWS, LANES), F32),
        compiler_params=_cparams(("arbitrary",)),
        name="experts",
    )(block_e, info, tok_tiles, w1, w3, w2)


def _combine_kernel(x1_ref, mod_ref, route_ref, f0_ref, f1_ref, g_ref, b_ref, o_ref, *, alpha):
    route = route_ref[0]
    f = route[:, 2:3] * _load_token_tiles(f0_ref, TM) + route[:, 3:4] * _load_token_tiles(f1_ref, TM)
    g2 = mod_ref[0, 0, 5:6, :]
    o_ref[0] = _layer_norm(alpha * x1_ref[0] + g2 * f, g_ref[...], b_ref[...])


def _combine(x1, modsel, route, ys, ln_g, ln_b, joff, nct, alpha):
    b, n, d = x1.shape
    nj = n // TM
    plane = b * nj

    def full(arr):
        return pl.BlockSpec(arr.shape, lambda bi, j: (0,) * arr.ndim)

    return pl.pallas_call(
        functools.partial(_combine_kernel, alpha=alpha),
        grid=(b, nj),
        in_specs=[pl.BlockSpec((1, TM, d), lambda bi, j: (bi, j, 0)),
                  pl.BlockSpec((1, 1, 6, d), lambda bi, j: (bi, jnp.minimum((j + joff) // nct, 1), 0, 0)),
                  pl.BlockSpec((1, TM, LANES), lambda bi, j: (bi, j, 0)),
                  pl.BlockSpec((TM * TILE_ROWS, LANES), lambda bi, j: (bi * nj + j, 0)),
                  pl.BlockSpec((TM * TILE_ROWS, LANES), lambda bi, j: (plane + bi * nj + j, 0)),
                  full(ln_g), full(ln_b)],
        out_specs=pl.BlockSpec((1, TM, d), lambda bi, j: (bi, j, 0)),
        out_shape=jax.ShapeDtypeStruct((b, n, d), F32),
        compiler_params=_cparams(("arbitrary", "arbitrary")),
        name="combine",
    )(x1, modsel, route, ys, ys, ln_g, ln_b)


def _rope_tables(seq, ctx_len):
    t = jnp.arange(seq, dtype=jnp.int32)
    row = (t // GRID_W).astype(F32)
    col = (t % GRID_W).astype(F32)

    def table(vec_dim):
        half = vec_dim // 4
        freqs = ROPE_THETA ** (-jnp.arange(half, dtype=F32) / half)
        lane = np.arange(vec_dim)
        idx = lane % half
        use_col = (lane // (2 * half)) == 1
        ang = jnp.where(use_col[None, :], col[:, None], row[:, None]) * freqs[idx][None, :]
        sign = np.where((lane % (2 * half)) < half, -1.0, 1.0).astype(np.float32)
        return jnp.cos(ang), jnp.sin(ang) * sign[None, :]

    def with_ctx(cos, sin):
        width = cos.shape[1]
        return (jnp.concatenate([jnp.ones((ctx_len, width), F32), cos], axis=0),
                jnp.concatenate([jnp.zeros((ctx_len, width), F32), sin], axis=0))

    c32, s32 = table(32)
    c64, s64 = table(64)
    cd, sd = with_ctx(jnp.tile(c32, (1, 4)), jnp.tile(s32, (1, 4)))
    cg, sg = with_ctx(jnp.tile(c64, (1, 2)), jnp.tile(s64, (1, 2)))
    ones, zeros = jnp.ones((seq, 64), F32), jnp.zeros((seq, 64), F32)
    cm, sm = with_ctx(jnp.concatenate([ones, c32, ones[:, :32]], axis=1),
                      jnp.concatenate([zeros, s32, zeros[:, :32]], axis=1))
    return cm, sm, cd, sd, cg, sg


def _in_proj_columns():
    def spread(start, odd_blocks):
        cols = []
        for i, odd in enumerate(odd_blocks):
            vals = list(range(start + 64 * i, start + 64 * (i + 1)))
            cols += ([-1] * 64 + vals) if odd else (vals + [-1] * 64)
        return cols

    cols = list(range(0, IN_CONV))
    cols += list(range(OFF_MLA, OFF_MLA + MLA_Q_LORA + MLA_KV_LORA))
    cols += [-1] * 64 + list(range(OFF_MLA + MLA_Q_LORA + MLA_KV_LORA, OFF_DIFF)) + [-1] * 32
    cols += list(range(OFF_DIFF, OFF_DIFF + 2 * DIFF_QW))
    cols += spread(OFF_DIFF + 2 * DIFF_QW, [h % 2 for h in range(DIFF_HEADS)])
    q0 = OFF_GQA
    head = lambda h: list(range(q0 + GQA_HD * h, q0 + GQA_HD * (h + 1)))
    cols += head(0) + head(2) + head(1) + head(3)
    cols += list(range(OFF_GQA + GQA_QW, OFF_GQA + GQA_QW + GQA_KW))
    v0 = OFF_GQA + GQA_QW + GQA_KW
    for kvh in range(GQA_KV_HEADS):
        vals = list(range(v0 + GQA_HD * kvh, v0 + GQA_HD * (kvh + 1)))
        cols += vals + [-1] * 64 + [-1] * 64 + vals
    assert len(cols) == P_END
    return np.asarray(cols, np.int32)


def _relayout_columns(w, cols):
    valid = jnp.asarray(cols >= 0)
    return jnp.where(valid[None, :], jnp.take(w, jnp.asarray(np.maximum(cols, 0)), axis=1), 0.0)


def _mla_weights(w_uq, w_qr, w_uk, w_uv):
    zq = jnp.zeros((MLA_Q_LORA, 32), F32)
    zk = jnp.zeros((MLA_KV_LORA, 64), F32)
    wq = jnp.concatenate([blk for h in range(MLA_HEADS)
                          for blk in (w_uq[:, 64 * h:64 * (h + 1)], w_qr[:, 32 * h:32 * (h + 1)], zq)], axis=1)
    wk = jnp.concatenate([blk for h in range(MLA_HEADS) for blk in (w_uk[:, 64 * h:64 * (h + 1)], zk)], axis=1)
    wv = jnp.concatenate([blk for h in range(MLA_HEADS)
                          for blk in ((zk, w_uv[:, 64 * h:64 * (h + 1)]) if h % 2 else
                                      (w_uv[:, 64 * h:64 * (h + 1)], zk))], axis=1)
    return wq.astype(BF16), jnp.concatenate([wk, wv], axis=1).astype(BF16)


def _slot_plan(route, counts, n_tok):
    n_slots = 2 * n_tok + N_EXPERTS * EB
    nblk = n_slots // EB
    cnt = counts.astype(jnp.int32)
    padded = (cnt + EB - 1) // EB * EB
    pad_ends = jnp.cumsum(padded)
    pad_starts = pad_ends - padded
    e = route[:, 0:2].astype(jnp.int32)
    rank = route[:, 4:6].astype(jnp.int32)
    dest = (pad_starts[e] + rank).reshape(-1)
    tok_id = jnp.arange(n_tok, dtype=jnp.int32)
    word = (tok_id[:, None] + jnp.array([0, 1 << 16], jnp.int32)[None, :]).reshape(-1)
    slot = jnp.arange(n_slots, dtype=jnp.int32)
    pad_word = (2 << 16) + ((slot // EB) % 2) * EB + slot % EB
    info = pad_word.at[dest].set(word)
    block_start = jnp.arange(nblk, dtype=jnp.int32) * EB
    block_e = jnp.minimum(jnp.sum((pad_ends[None, :] <= block_start[:, None]).astype(jnp.int32), axis=1),
                          N_EXPERTS - 1)
    return block_e, info


def kernel(x, c, ctx, c_ctx, w_ada, b_ada, w_in, w_out, conv_w, mla_q_norm, mla_kv_norm, mla_w_uq, mla_w_qr,
           mla_w_uk, mla_w_uv, diff_lambda, diff_subln, gqa_q_norm, gqa_k_norm, ln1_g, ln1_b, ln2_g, ln2_b,
           moe_w_group, moe_b_group, moe_w_sub, moe_b_sub, moe_w1, moe_w3, moe_w2):
    b, s, d = x.shape
    n_ctx = ctx.shape[1]
    depth = w_in.shape[0]
    assert n_ctx % TM == 0 and s % TM == 0 and s % GRID_W == 0 and d == SUBLANES * LANES
    nct = n_ctx // TM
    t = n_ctx + s
    alpha = (2 * depth) ** 0.25

    rows = ((b + 1 + SUBLANES - 1) // SUBLANES) * SUBLANES
    cc = jnp.zeros((rows, d), F32).at[:b].set(c).at[b].set(c_ctx)
    mod = _modulation(cc, w_ada, b_ada)
    tabs = _rope_tables(s, n_ctx)
    cols = _in_proj_columns()
    xcat = jnp.concatenate([ctx, x], axis=1)

    for l in range(depth):
        last = l == depth - 1
        joff = nct if last else 0
        lam_init = 0.8 - 0.6 * math.exp(-0.3 * l)
        ml = mod[l].reshape(rows, 6, d)
        modsel = jnp.stack([jnp.broadcast_to(ml[b], (b, 6, d)), ml[:b]], axis=1)
        win_p = _relayout_columns(w_in[l], cols).astype(BF16)
        wq, wkv = _mla_weights(mla_w_uq[l], mla_w_qr[l], mla_w_uk[l], mla_w_uv[l])
        gq = jnp.tile(gqa_q_norm[l], 2)[None, :]
        gk = jnp.tile(gqa_k_norm[l], 2)[None, :]
        gb, zc, qm, km, vm, qd, kd, vd, qg, kg, vg = _proj(
            xcat, modsel, win_p, tabs, mla_q_norm[l][None, :], mla_kv_norm[l][None, :], wq, wkv, gq, gk, nct)

        ym = _attention(qm, km, vm, (), _mla_heads, joff, nct, "attn_mla")
        yd = _attention(qd, kd, vd, (diff_lambda[l], jnp.tile(diff_subln[l], 2)[None, :]),
                        functools.partial(_diff_heads, lam_init), joff, nct, "attn_diff")
        yg = _attention(qg, kg, vg, (), _gqa_heads, joff, nct, "attn_gqa")

        wr = jnp.zeros((d, LANES), F32).at[:, :N_EXPERTS].set(moe_w_sub[l])
        wr = wr.at[:, N_EXPERTS:N_EXPERTS + N_GROUPS].set(moe_w_group[l])
        wr_hi = wr.astype(BF16)
        wr = jnp.concatenate([wr_hi, (wr - wr_hi.astype(F32)).astype(BF16)], axis=1)
        br = jnp.zeros((1, LANES), F32).at[0, :N_EXPERTS].set(moe_b_sub[l])
        br = br.at[0, N_EXPERTS:N_EXPERTS + N_GROUPS].set(moe_b_group[l])
        x1, tok, route, counts = _post(xcat, modsel, gb, zc, ym, yd, yg, conv_w[l], w_out[l].astype(BF16),
                                       ln1_g[l][None, :], ln1_b[l][None, :], wr, br, joff, nct, alpha)

        n_tok = x1.shape[0] * x1.shape[1]
        block_e, info = _slot_plan(route.reshape(n_tok, LANES), counts[0, :N_EXPERTS], n_tok)
        ys = _experts(block_e, info, tok, moe_w1[l], moe_w3[l], moe_w2[l])
        xcat = _combine(x1, modsel, route, ys, ln2_g[l][None, :], ln2_b[l][None, :], joff, nct, alpha)
    return xcat
```

```python
import functools
import math

import jax
import jax.numpy as jnp
import numpy as np
from jax import lax
from jax.experimental import pallas as pl
from jax.experimental.pallas import tpu as pltpu

F32 = jnp.float32
BF16 = jnp.bfloat16

GRID_W = 64
ROPE_THETA = 10000.0
NORM_EPS = 1e-6
CONV_CH = 256
MLA_HEADS, MLA_Q_LORA, MLA_KV_LORA, MLA_NOPE, MLA_ROPE, MLA_V = 4, 256, 128, 64, 32, 64
DIFF_HEADS, DIFF_QK, DIFF_V = 4, 32, 64
GQA_Q_HEADS, GQA_KV_HEADS, GQA_HD = 4, 2, 64
IN_CONV = 3 * CONV_CH
IN_MLA = MLA_Q_LORA + MLA_KV_LORA + MLA_ROPE
DIFF_QW = DIFF_HEADS * 2 * DIFF_QK
IN_DIFF = 2 * DIFF_QW + DIFF_HEADS * DIFF_V
GQA_QW = GQA_Q_HEADS * GQA_HD
GQA_KW = GQA_KV_HEADS * GQA_HD
OFF_MLA = IN_CONV
OFF_DIFF = OFF_MLA + IN_MLA
OFF_GQA = OFF_DIFF + IN_DIFF
MLA_SCALE = (MLA_NOPE + MLA_ROPE) ** -0.5
DIFF_SCALE = DIFF_QK ** -0.5
GQA_SCALE = GQA_HD ** -0.5
N_GROUPS, EXPERTS_PER_GROUP = 4, 8
N_EXPERTS = N_GROUPS * EXPERTS_PER_GROUP
D_EXPERT = 256
LOG2E = math.log2(math.e)

LANES = 128
SUBLANES = 8
TM = 256
EB = 256
VMEM_LIMIT = 56 * 1024 * 1024

P_CONV = 0
P_MLA = P_CONV + 768
P_DIFF = P_MLA + 512
P_GQA = P_DIFF + 1024
P_END = P_GQA + 896


def _cparams(sem, flags=None):
    return pltpu.CompilerParams(dimension_semantics=sem, vmem_limit_bytes=VMEM_LIMIT, flags=flags)


def _lane(shape):
    return lax.broadcasted_iota(jnp.int32, shape, len(shape) - 1)


def _mod_kernel(c_ref, w_ref, b_ref, o_ref):
    c = c_ref[...]
    sc = (c * jax.nn.sigmoid(c)).astype(BF16)
    o_ref[0] = jnp.dot(sc, w_ref[0].astype(BF16), preferred_element_type=F32) + b_ref[0]


def _modulation(cc, w_ada, b_ada):
    n_layers, d, d6 = w_ada.shape
    r = cc.shape[0]
    return pl.pallas_call(
        _mod_kernel,
        grid=(n_layers, d6 // d),
        in_specs=[pl.BlockSpec((r, d), lambda l, n: (0, 0)),
                  pl.BlockSpec((1, d, d), lambda l, n: (l, 0, n)),
                  pl.BlockSpec((1, 1, d), lambda l, n: (l, 0, n))],
        out_specs=pl.BlockSpec((1, r, d), lambda l, n: (l, 0, n)),
        out_shape=jax.ShapeDtypeStruct((n_layers, r, d6), F32),
        compiler_params=_cparams(("arbitrary", "arbitrary")),
        name="mod",
    )(cc, w_ada, b_ada.reshape(n_layers, 1, d6))


def _rope(z, cos, sin_signed, half):
    n = z.shape[-1]
    first = (_lane(z.shape) % (2 * half)) < half
    partner = jnp.where(first, pltpu.roll(z, n - half, 1), pltpu.roll(z, half, 1))
    return z * cos + partner * sin_signed


def _rms(z, gain):
    return z * lax.rsqrt(jnp.mean(z * z, axis=-1, keepdims=True) + NORM_EPS) * gain


def _seg_rms64(z, gain):
    lo = _lane(z.shape) < 64
    z2 = z * z
    s_lo = jnp.sum(jnp.where(lo, z2, 0.0), axis=-1, keepdims=True)
    s_hi = jnp.sum(jnp.where(lo, 0.0, z2), axis=-1, keepdims=True)
    ms = jnp.where(lo, s_lo, s_hi) * (1.0 / 64)
    return z * lax.rsqrt(ms + NORM_EPS) * gain


def _ones_lane(odd):
    return (_lane((1, LANES)) == (0 if odd else 64)).astype(F32)


def _stream_tile(xa_ref, xb_ref, jj, nct):
    return jnp.where(jj < nct, xa_ref[0], xb_ref[0])


def _stream_specs(stream, joff, nct):
    xa, xb, lat_off = stream
    d = xa.shape[2]
    return [pl.BlockSpec((1, TM, d), lambda bi, j: (bi, jnp.minimum(j + joff, nct - 1), 0)),
            pl.BlockSpec((1, TM, d), lambda bi, j: (bi, jnp.maximum(j + joff - nct, 0) + lat_off, 0))]


def _proj_kernel(xa_ref, xb_ref, mod_ref, win_ref, cm_ref, sm_ref, cd_ref, sd_ref, cg_ref, sg_ref,
                 qn_ref, kvn_ref, wq_ref, wkv_ref, gq_ref, gk_ref,
                 gb_ref, zc_ref, qm_ref, km_ref, vm_ref, qd_ref, kd_ref, vd_ref, qg_ref, kg_ref, vg_ref, *, nct):
    x = _stream_tile(xa_ref, xb_ref, pl.program_id(1), nct)
    sh1 = mod_ref[0, 0, 0:1, :]
    sc1 = mod_ref[0, 0, 1:2, :]
    xm = (x * (1.0 + sc1) + sh1).astype(BF16)

    uc = jnp.dot(xm, win_ref[:, P_CONV:P_MLA], preferred_element_type=F32)
    gb_ref[0] = uc[:, 0:256]
    zc_ref[0] = uc[:, 256:512] * uc[:, 512:768]

    um = jnp.dot(xm, win_ref[:, P_MLA:P_DIFF], preferred_element_type=F32)
    cm, sm = cm_ref[...], sm_ref[...]
    cq = _rms(um[:, 0:256], qn_ref[...]).astype(BF16)
    qall = jnp.dot(cq, wq_ref[...], preferred_element_type=F32)
    ckv = _rms(um[:, 256:384], kvn_ref[...]).astype(BF16)
    kv = jnp.dot(ckv, wkv_ref[...], preferred_element_type=F32)
    kr = _rope(um[:, 384:512], cm, sm, 8)
    for h in range(MLA_HEADS):
        sl = slice(LANES * h, LANES * (h + 1))
        qm_ref[0, :, sl] = (_rope(qall[:, sl], cm, sm, 8) * (MLA_SCALE * LOG2E)).astype(BF16)
        km_ref[0, :, sl] = (kv[:, sl] + kr).astype(BF16)
        vm_ref[0, :, sl] = (kv[:, 512 + LANES * h:512 + LANES * (h + 1)] + _ones_lane(h % 2)).astype(BF16)

    ud = jnp.dot(xm, win_ref[:, P_DIFF:P_GQA], preferred_element_type=F32)
    cd, sd = cd_ref[...], sd_ref[...]
    for c in range(2):
        sl = slice(LANES * c, LANES * (c + 1))
        qd_ref[0, :, sl] = (_rope(ud[:, sl], cd, sd, 8) * (DIFF_SCALE * LOG2E)).astype(BF16)
        kd_ref[0, :, sl] = _rope(ud[:, 256 + LANES * c:256 + LANES * (c + 1)], cd, sd, 8).astype(BF16)
    for h in range(DIFF_HEADS):
        sl = slice(LANES * h, LANES * (h + 1))
        vd_ref[0, :, sl] = (ud[:, 512 + LANES * h:512 + LANES * (h + 1)] + _ones_lane(h % 2)).astype(BF16)

    ug = jnp.dot(xm, win_ref[:, P_GQA:P_END], preferred_element_type=F32)
    cg, sg = cg_ref[...], sg_ref[...]
    for c in range(2):
        sl = slice(LANES * c, LANES * (c + 1))
        qg_ref[0, :, sl] = (_rope(_seg_rms64(ug[:, sl], gq_ref[...]), cg, sg, 16) * (GQA_SCALE * LOG2E)).astype(BF16)
    kg_ref[0] = _rope(_seg_rms64(ug[:, 256:384], gk_ref[...]), cg, sg, 16).astype(BF16)
    for i in range(4):
        sl = slice(LANES * i, LANES * (i + 1))
        vg_ref[0, :, sl] = (ug[:, 384 + LANES * i:384 + LANES * (i + 1)] + _ones_lane(i % 2)).astype(BF16)


def _proj(stream, t, modsel, win_p, tabs, qn, kvn, wq, wkv, gq, gk, nct):
    b, _, d = stream[0].shape
    nt = t // TM

    def tile(width):
        return pl.BlockSpec((1, TM, width), lambda bi, j: (bi, j, 0))

    def full(arr):
        return pl.BlockSpec(arr.shape, lambda bi, j: (0,) * arr.ndim)

    tab_spec = pl.BlockSpec((TM, LANES), lambda bi, j: (j, 0))
    widths = [(256, F32), (256, F32), (512, BF16), (512, BF16), (512, BF16), (256, BF16), (256, BF16),
              (512, BF16), (256, BF16), (128, BF16), (512, BF16)]
    return pl.pallas_call(
        functools.partial(_proj_kernel, nct=nct),
        grid=(b, nt),
        in_specs=_stream_specs(stream, 0, nct) + [
                  pl.BlockSpec((1, 1, 6, d), lambda bi, j: (bi, jnp.minimum(j // nct, 1), 0, 0)),
                  full(win_p)] + [tab_spec] * 6 + [full(a) for a in (qn, kvn, wq, wkv, gq, gk)],
        out_specs=[tile(w) for w, _ in widths],
        out_shape=[jax.ShapeDtypeStruct((b, t, w), dt) for w, dt in widths],
        compiler_params=_cparams(("arbitrary", "arbitrary")),
        name="proj",
    )(stream[0], stream[1], modsel, win_p, *tabs, qn, kvn, wq, wkv, gq, gk)


def _scores(q_ref, k_ref, tk, chunk, mask):
    q = q_ref[0, :, LANES * chunk:LANES * (chunk + 1)]
    if mask is not None:
        lane = _lane(q.shape)
        q = jnp.where((lane >= mask[0]) & (lane < mask[1]), q, jnp.zeros_like(q))
    k = k_ref[0, 0:tk, LANES * chunk:LANES * (chunk + 1)]
    return lax.dot_general(q, k, (((1,), (1,)), ((), ())), preferred_element_type=F32)


def _attend(s, v, odd):
    p = jnp.exp2(s - jnp.max(s, axis=-1, keepdims=True)).astype(BF16)
    o = jnp.dot(p, v, preferred_element_type=F32)
    den = o[:, 0:1] if odd else o[:, 64:65]
    return o * (1.0 / den)


def _pair(even, odd):
    return jnp.where(_lane(even.shape) < 64, even, odd)


def _vblock(v_ref, tk, idx):
    return v_ref[0, 0:tk, LANES * idx:LANES * (idx + 1)]


def _one_ahead(n, score_fn, finish_fn):
    outs, nxt = [], score_fn(0)
    for h in range(n):
        cur = nxt
        if h + 1 < n:
            nxt = score_fn(h + 1)
        outs.append(finish_fn(h, cur))
    return outs


def _mla_heads(q_ref, k_ref, v_ref, tk, extra):
    o = _one_ahead(MLA_HEADS, lambda h: _scores(q_ref, k_ref, tk, h, None),
                   lambda h, s: _attend(s, _vblock(v_ref, tk, h), h % 2))
    return jnp.concatenate([_pair(o[0], o[1]), _pair(o[2], o[3])], axis=-1)


def _gqa_heads(q_ref, k_ref, v_ref, tk, extra):
    def scores(i):
        kvh, g = divmod(i, GQA_Q_HEADS // GQA_KV_HEADS)
        q = q_ref[0, :, LANES * g:LANES * (g + 1)]
        lane = _lane(q.shape)
        q = jnp.where((lane >= 64 * kvh) & (lane < 64 * kvh + 64), q, jnp.zeros_like(q))
        return lax.dot_general(q, k_ref[0, 0:tk, :], (((1,), (1,)), ((), ())), preferred_element_type=F32)

    o = _one_ahead(GQA_Q_HEADS, scores, lambda i, s: _attend(s, _vblock(v_ref, tk, i), i % 2))
    return jnp.concatenate([_pair(o[0], o[1]), _pair(o[2], o[3])], axis=-1)


def _diff_heads(lam_init, q_ref, k_ref, v_ref, tk, extra):
    dl_ref, subln_ref = extra
    dl = dl_ref[...]
    lam = (jnp.exp(jnp.sum(dl[0:1] * dl[1:2], axis=-1, keepdims=True))
           - jnp.exp(jnp.sum(dl[2:3] * dl[3:4], axis=-1, keepdims=True)) + lam_init)

    def head_scores(h):
        return [_scores(q_ref, k_ref, tk, (2 * h + m) // 4, (32 * ((2 * h + m) % 4), 32 * ((2 * h + m) % 4) + 32))
                for m in range(2)]

    def finish(h, pair):
        probs = []
        for s in pair:
            e = jnp.exp2(s - jnp.max(s, axis=-1, keepdims=True))
            probs.append((e, jnp.sum(e, axis=-1, keepdims=True)))
        a = probs[0][0] * (1.0 / probs[0][1]) - probs[1][0] * (lam / probs[1][1])
        return jnp.dot(a.astype(BF16), _vblock(v_ref, tk, h), preferred_element_type=F32)

    d = _one_ahead(DIFF_HEADS, head_scores, finish)
    chunks = [_seg_rms64(_pair(d[2 * c], d[2 * c + 1]), subln_ref[...]) * (1.0 - lam_init) for c in range(2)]
    return jnp.concatenate(chunks, axis=-1)


def _attn_kernel(*refs, heads, joff, nct, ctx_len, total_len):
    q_ref, k_ref, v_ref = refs[:3]
    extra, o_ref = refs[3:-1], refs[-1]
    jj = pl.program_id(1) + joff

    def run(tk):
        o_ref[0] = heads(q_ref, k_ref, v_ref, tk, extra).astype(o_ref.dtype)

    if joff < nct:
        pl.when(jj < nct)(lambda: run(ctx_len))
        pl.when(jj >= nct)(lambda: run(total_len))
    else:
        run(total_len)


def _attention(q, k, v, extra, heads, joff, nct, name):
    b, t, _ = q.shape
    nt = t // TM

    def full(arr):
        return pl.BlockSpec(arr.shape, lambda bi, j: (0,) * arr.ndim)

    kern = functools.partial(_attn_kernel, heads=heads, joff=joff, nct=nct, ctx_len=nct * TM, total_len=t)
    return pl.pallas_call(
        kern,
        grid=(b, nt - joff),
        in_specs=[pl.BlockSpec((1, TM, q.shape[2]), lambda bi, j: (bi, j + joff, 0)),
                  pl.BlockSpec((1, t, k.shape[2]), lambda bi, j: (bi, 0, 0)),
                  pl.BlockSpec((1, t, v.shape[2]), lambda bi, j: (bi, 0, 0))] + [full(a) for a in extra],
        out_specs=pl.BlockSpec((1, TM, 256), lambda bi, j: (bi, j, 0)),
        out_shape=jax.ShapeDtypeStruct((b, t - joff * TM, 256), BF16),
        compiler_params=_cparams(("arbitrary", "arbitrary")),
        name=name,
    )(q, k, v, *extra)


def _layer_norm(h, g, bias):
    mu = jnp.mean(h, axis=-1, keepdims=True)
    hc = h - mu
    var = jnp.mean(hc * hc, axis=-1, keepdims=True)
    return hc * lax.rsqrt(var + NORM_EPS) * g + bias


def _store_token_tiles(ref, val):
    rows = val.shape[0]
    for c in range(SUBLANES):
        ref[pl.ds(c, rows, stride=SUBLANES), :] = val[:, LANES * c:LANES * (c + 1)]


def _load_token_tiles(ref, rows):
    return jnp.concatenate([ref[pl.ds(c, rows, stride=SUBLANES), :] for c in range(SUBLANES)], axis=1)


def _min_lane(cond, lane_f):
    return jnp.min(jnp.where(cond, lane_f, float(LANES)), axis=-1, keepdims=True)


def _post_kernel(xa_ref, xb_ref, mod_ref, gb_ref, zc_ref, zp_ref, zn_ref, ym_ref, yd_ref, yg_ref, cw_ref, wout_ref,
                 g_ref, b_ref, wr_ref, br_ref, x1_ref, tok_ref, route_ref, cnt_ref, cnt_acc,
                 *, joff, nct, nt, alpha):
    jj = pl.program_id(1) + joff
    first_step = (pl.program_id(0) == 0) & (pl.program_id(1) == 0)

    @pl.when(first_step)
    def _():
        cnt_acc[...] = jnp.zeros_like(cnt_acc)

    zc = zc_ref[0]
    row = lax.broadcasted_iota(jnp.int32, zc.shape, 0)
    left_ok = (jj != 0) & (jj != nct)
    right_ok = (jj != nct - 1) & (jj != nt - 1)
    halo_prev = jnp.where(left_ok, zp_ref[0, SUBLANES - 1:SUBLANES, :], 0.0)
    halo_next = jnp.where(right_ok, zn_ref[0, 0:1, :], 0.0)
    zprev = jnp.where(row == 0, halo_prev, pltpu.roll(zc, 1, 0))
    znext = jnp.where(row == TM - 1, halo_next, pltpu.roll(zc, TM - 1, 0))
    cw = cw_ref[...]
    conv = zprev * cw[0:1] + zc * cw[1:2] + znext * cw[2:3]
    yc = (gb_ref[0] * conv).astype(BF16)

    y = jnp.concatenate([yc, ym_ref[0], yd_ref[0], yg_ref[0]], axis=-1)
    acc = jnp.dot(y, wout_ref[...], preferred_element_type=F32)

    g1 = mod_ref[0, 0, 2:3, :]
    sh2 = mod_ref[0, 0, 3:4, :]
    sc2 = mod_ref[0, 0, 4:5, :]
    x1 = _layer_norm(alpha * _stream_tile(xa_ref, xb_ref, jj, nct) + g1 * acc, g_ref[...], b_ref[...])
    x1_ref[0] = x1
    tok = x1 * (1.0 + sc2) + sh2
    _store_token_tiles(tok_ref, tok)

    tok_hi = tok.astype(BF16)
    tok_lo = (tok - tok_hi.astype(F32)).astype(BF16)
    part = jnp.dot(tok_hi, wr_ref[...], preferred_element_type=F32)
    logits = (part[:, 0:LANES] + (part[:, LANES:2 * LANES]
                                  + jnp.dot(tok_lo, wr_ref[:, 0:LANES], preferred_element_type=F32))) + br_ref[...]
    lane = _lane(logits.shape)
    lane_f = lane.astype(F32)
    neg = -jnp.inf
    is_g = (lane >= N_EXPERTS) & (lane < N_EXPERTS + N_GROUPS)
    lg = jnp.where(is_g, logits, neg)
    eg = jnp.exp(lg - jnp.max(lg, axis=-1, keepdims=True))
    pg = eg / jnp.sum(eg, axis=-1, keepdims=True)
    g_w = jnp.max(pg, axis=-1, keepdims=True)
    g_lane = _min_lane(is_g & (pg == g_w), lane_f)
    g_idx = g_lane - float(N_EXPERTS)
    in_grp = (lane < N_EXPERTS) & ((lane // EXPERTS_PER_GROUP).astype(F32) == g_idx)
    ls = jnp.where(in_grp, logits, neg)
    es = jnp.exp(ls - jnp.max(ls, axis=-1, keepdims=True))
    ps = es / jnp.sum(es, axis=-1, keepdims=True)
    p1 = jnp.max(jnp.where(in_grp, ps, -1.0), axis=-1, keepdims=True)
    i1 = _min_lane(in_grp & (ps == p1), lane_f)
    rest = in_grp & (lane_f != i1)
    p2 = jnp.max(jnp.where(rest, ps, -1.0), axis=-1, keepdims=True)
    i2 = _min_lane(rest & (ps == p2), lane_f)
    tot = p1 + p2
    w1 = g_w * (p1 / tot)
    w2 = g_w * (p2 / tot)

    oh1 = lane_f == i1
    oh2 = lane_f == i2
    r_i = lax.broadcasted_iota(jnp.int32, (TM, TM), 0)
    c_i = lax.broadcasted_iota(jnp.int32, (TM, TM), 1)
    tri = (r_i > c_i).astype(BF16)
    before1 = jnp.dot(tri, oh1.astype(BF16), preferred_element_type=F32)
    before2 = jnp.dot(tri, oh2.astype(BF16), preferred_element_type=F32)
    tot1 = jnp.sum(oh1.astype(F32), axis=0, keepdims=True)
    tot2 = jnp.sum(oh2.astype(F32), axis=0, keepdims=True)
    base = cnt_acc[...]
    r1 = jnp.sum(jnp.where(oh1, base + before1, 0.0), axis=-1, keepdims=True)
    r2 = jnp.sum(jnp.where(oh2, base + tot1 + before2, 0.0), axis=-1, keepdims=True)
    new_cnt = base + tot1 + tot2
    cnt_acc[...] = new_cnt
    cnt_ref[...] = jnp.broadcast_to(new_cnt, cnt_ref.shape)

    vals = (i1, i2, w1, w2, r1, r2)
    route = jnp.zeros(logits.shape, F32)
    for idx, val in enumerate(vals):
        route = jnp.where(lane == idx, val, route)
    route_ref[0] = route


def _post(stream, t, modsel, gb, zc, ym, yd, yg, conv_w, wout, ln_g, ln_b, wr, br, joff, nct, alpha):
    b, _, d = stream[0].shape
    nt = t // TM
    nj = nt - joff
    hb = TM // SUBLANES

    def tile(width):
        return pl.BlockSpec((1, TM, width), lambda bi, j: (bi, j + joff, 0))

    def otile(width):
        return pl.BlockSpec((1, TM, width), lambda bi, j: (bi, j, 0))

    def full(arr):
        return pl.BlockSpec(arr.shape, lambda bi, j: (0,) * arr.ndim)

    kern = functools.partial(_post_kernel, joff=joff, nct=nct, nt=nt, alpha=alpha)
    return pl.pallas_call(
        kern,
        grid=(b, nj),
        in_specs=_stream_specs(stream, joff, nct) + [
                  pl.BlockSpec((1, 1, 6, d), lambda bi, j: (bi, jnp.minimum((j + joff) // nct, 1), 0, 0)),
                  tile(256), tile(256),
                  pl.BlockSpec((1, SUBLANES, 256), lambda bi, j: (bi, jnp.maximum((j + joff) * hb - 1, 0), 0)),
                  pl.BlockSpec((1, SUBLANES, 256),
                               lambda bi, j: (bi, jnp.minimum((j + joff + 1) * hb, nt * hb - 1), 0)),
                  otile(256), otile(256), otile(256),
                  full(conv_w), full(wout), full(ln_g), full(ln_b), full(wr), full(br)],
        out_specs=[otile(d), pl.BlockSpec((TM * SUBLANES, LANES), lambda bi, j: (bi * nj + j, 0)),
                   otile(LANES), pl.BlockSpec((SUBLANES, LANES), lambda bi, j: (0, 0))],
        out_shape=[jax.ShapeDtypeStruct((b, nj * TM, d), F32),
                   jax.ShapeDtypeStruct((b * nj * TM * SUBLANES, LANES), F32),
                   jax.ShapeDtypeStruct((b, nj * TM, LANES), F32),
                   jax.ShapeDtypeStruct((SUBLANES, LANES), F32)],
        scratch_shapes=[pltpu.VMEM((1, LANES), F32)],
        compiler_params=_cparams(("arbitrary", "arbitrary")),
        name="post",
    )(stream[0], stream[1], modsel, gb, zc, zc, zc, ym, yd, yg, conv_w, wout, ln_g, ln_b, wr, br)


ROW_UNROLL = EB
TILE_ROWS = SUBLANES


def _expert_kernel(be_ref, offs_ref, tok_ref, w1_ref, w3_ref, w2_ref, out_ref, xbuf, ybuf, gsem, ssem,
                   *, n_tok, nblk):
    del be_ref
    i = pl.program_id(0)
    slot = i % 2
    blk_rows = EB * TILE_ROWS

    def tile_at(ref, first_row):
        return ref.at[pl.ds(pl.multiple_of(first_row, TILE_ROWS), TILE_ROWS)]

    def gather_start(blk, sl):
        def body(g, carry):
            for u in range(ROW_UNROLL):
                r = g * ROW_UNROLL + u
                pltpu.make_async_copy(tile_at(tok_ref, offs_ref[blk, r]), tile_at(xbuf.at[sl], r * TILE_ROWS),
                                      gsem.at[sl]).start(priority=u % 2)
            return carry
        lax.fori_loop(0, EB // ROW_UNROLL, body, 0)

    def scatter_start(blk, sl):
        def body(g, carry):
            for u in range(ROW_UNROLL):
                r = g * ROW_UNROLL + u
                pltpu.make_async_copy(tile_at(ybuf.at[sl], r * TILE_ROWS), tile_at(out_ref, offs_ref[blk, EB + r]),
                                      ssem.at[sl]).start(priority=u % 2)
            return carry
        lax.fori_loop(0, EB // ROW_UNROLL, body, 0)

    def gather_wait(sl):
        pltpu.make_async_copy(tok_ref.at[pl.ds(0, blk_rows)], xbuf.at[sl], gsem.at[sl]).wait()

    def scatter_wait(sl):
        pltpu.make_async_copy(ybuf.at[sl], out_ref.at[pl.ds(0, blk_rows)], ssem.at[sl]).wait()

    @pl.when(i == 0)
    def _():
        gather_start(0, 0)

    @pl.when(i + 1 < nblk)
    def _():
        gather_start(i + 1, 1 - slot)

    gather_wait(slot)

    @pl.when(i >= 2)
    def _():
        scatter_wait(slot)

    xb = _load_token_tiles(xbuf.at[slot], EB).astype(BF16)
    h1 = jnp.dot(xb, w1_ref[0, 0].astype(BF16), preferred_element_type=F32)
    h3 = jnp.dot(xb, w3_ref[0, 0].astype(BF16), preferred_element_type=F32)
    a = (h1 * jax.nn.sigmoid(h1) * h3).astype(BF16)
    _store_token_tiles(ybuf.at[slot], jnp.dot(a, w2_ref[0, 0].astype(BF16), preferred_element_type=F32))
    scatter_start(i, slot)

    @pl.when(i == nblk - 1)
    def _():
        scatter_wait(1 - slot)
        scatter_wait(slot)
        ybuf[0] = jnp.zeros(ybuf.shape[1:], F32)
        for half in range(2):
            cp = pltpu.make_async_copy(
                ybuf.at[0], out_ref.at[pl.ds((2 * n_tok + half * EB) * TILE_ROWS, blk_rows)], ssem.at[0])
            cp.start()
            cp.wait()


def _experts(block_e, offs, tok_tiles, w1, w3, w2, layer):
    n_tok = tok_tiles.shape[0] // TILE_ROWS
    d, de = w1.shape[2], w1.shape[3]
    nblk = offs.shape[0]
    assert nblk >= 2 and d == TILE_ROWS * LANES
    kern = functools.partial(_expert_kernel, n_tok=n_tok, nblk=nblk)
    return pl.pallas_call(
        kern,
        grid_spec=pltpu.PrefetchScalarGridSpec(
            num_scalar_prefetch=2,
            grid=(nblk,),
            in_specs=[pl.BlockSpec(memory_space=pl.ANY),
                      pl.BlockSpec((1, 1, d, de), lambda i, be, offs: (layer, be[i], 0, 0)),
                      pl.BlockSpec((1, 1, d, de), lambda i, be, offs: (layer, be[i], 0, 0)),
                      pl.BlockSpec((1, 1, de, d), lambda i, be, offs: (layer, be[i], 0, 0))],
            out_specs=pl.BlockSpec(memory_space=pl.ANY),
            scratch_shapes=[pltpu.VMEM((2, EB * TILE_ROWS, LANES), F32), pltpu.VMEM((2, EB * TILE_ROWS, LANES), F32),
                            pltpu.SemaphoreType.DMA((2,)), pltpu.SemaphoreType.DMA((2,))]),
        out_shape=jax.ShapeDtypeStruct(((2 * n_tok + 2 * EB) * TILE_ROWS, LANES), F32),
        compiler_params=_cparams(("arbitrary",)),
        name="experts",
    )(block_e, offs, tok_tiles, w1, w3, w2)


def _combine_kernel(x1_ref, mod_ref, route_ref, f0_ref, f1_ref, g_ref, b_ref, o_ref, *, alpha):
    route = route_ref[0]
    f = route[:, 2:3] * _load_token_tiles(f0_ref, TM) + route[:, 3:4] * _load_token_tiles(f1_ref, TM)
    g2 = mod_ref[0, 0, 5:6, :]
    o_ref[0] = _layer_norm(alpha * x1_ref[0] + g2 * f, g_ref[...], b_ref[...])


def _combine(x1, modsel, route, ys, ln_g, ln_b, joff, nct, alpha):
    b, n, d = x1.shape
    nj = n // TM
    plane = b * nj

    def full(arr):
        return pl.BlockSpec(arr.shape, lambda bi, j: (0,) * arr.ndim)

    return pl.pallas_call(
        functools.partial(_combine_kernel, alpha=alpha),
        grid=(b, nj),
        in_specs=[pl.BlockSpec((1, TM, d), lambda bi, j: (bi, j, 0)),
                  pl.BlockSpec((1, 1, 6, d), lambda bi, j: (bi, jnp.minimum((j + joff) // nct, 1), 0, 0)),
                  pl.BlockSpec((1, TM, LANES), lambda bi, j: (bi, j, 0)),
                  pl.BlockSpec((TM * TILE_ROWS, LANES), lambda bi, j: (bi * nj + j, 0)),
                  pl.BlockSpec((TM * TILE_ROWS, LANES), lambda bi, j: (plane + bi * nj + j, 0)),
                  full(ln_g), full(ln_b)],
        out_specs=pl.BlockSpec((1, TM, d), lambda bi, j: (bi, j, 0)),
        out_shape=jax.ShapeDtypeStruct((b, n, d), F32),
        compiler_params=_cparams(("arbitrary", "arbitrary")),
        name="combine",
    )(x1, modsel, route, ys, ys, ln_g, ln_b)


def _rope_tables(seq, ctx_len):
    t = jnp.arange(seq, dtype=jnp.int32)
    row = (t // GRID_W).astype(F32)
    col = (t % GRID_W).astype(F32)

    def table(vec_dim):
        half = vec_dim // 4
        freqs = ROPE_THETA ** (-jnp.arange(half, dtype=F32) / half)
        lane = np.arange(vec_dim)
        idx = lane % half
        use_col = (lane // (2 * half)) == 1
        ang = jnp.where(use_col[None, :], col[:, None], row[:, None]) * freqs[idx][None, :]
        sign = np.where((lane % (2 * half)) < half, -1.0, 1.0).astype(np.float32)
        return jnp.cos(ang), jnp.sin(ang) * sign[None, :]

    def with_ctx(cos, sin):
        width = cos.shape[1]
        return (jnp.concatenate([jnp.ones((ctx_len, width), F32), cos], axis=0),
                jnp.concatenate([jnp.zeros((ctx_len, width), F32), sin], axis=0))

    c32, s32 = table(32)
    c64, s64 = table(64)
    cd, sd = with_ctx(jnp.tile(c32, (1, 4)), jnp.tile(s32, (1, 4)))
    cg, sg = with_ctx(jnp.tile(c64, (1, 2)), jnp.tile(s64, (1, 2)))
    ones, zeros = jnp.ones((seq, 64), F32), jnp.zeros((seq, 64), F32)
    cm, sm = with_ctx(jnp.concatenate([ones, c32, ones[:, :32]], axis=1),
                      jnp.concatenate([zeros, s32, zeros[:, :32]], axis=1))
    return cm, sm, cd, sd, cg, sg


def _in_proj_columns():
    def spread(start, odd_blocks):
        cols = []
        for i, odd in enumerate(odd_blocks):
            vals = list(range(start + 64 * i, start + 64 * (i + 1)))
            cols += ([-1] * 64 + vals) if odd else (vals + [-1] * 64)
        return cols

    cols = list(range(0, IN_CONV))
    cols += list(range(OFF_MLA, OFF_MLA + MLA_Q_LORA + MLA_KV_LORA))
    cols += [-1] * 64 + list(range(OFF_MLA + MLA_Q_LORA + MLA_KV_LORA, OFF_DIFF)) + [-1] * 32
    cols += list(range(OFF_DIFF, OFF_DIFF + 2 * DIFF_QW))
    cols += spread(OFF_DIFF + 2 * DIFF_QW, [h % 2 for h in range(DIFF_HEADS)])
    q0 = OFF_GQA
    head = lambda h: list(range(q0 + GQA_HD * h, q0 + GQA_HD * (h + 1)))
    cols += head(0) + head(2) + head(1) + head(3)
    cols += list(range(OFF_GQA + GQA_QW, OFF_GQA + GQA_QW + GQA_KW))
    v0 = OFF_GQA + GQA_QW + GQA_KW
    for kvh in range(GQA_KV_HEADS):
        vals = list(range(v0 + GQA_HD * kvh, v0 + GQA_HD * (kvh + 1)))
        cols += vals + [-1] * 64 + [-1] * 64 + vals
    assert len(cols) == P_END
    return np.asarray(cols, np.int32)


def _relayout_columns(w, cols):
    valid = jnp.asarray(cols >= 0)
    return jnp.where(valid[None, :], jnp.take(w, jnp.asarray(np.maximum(cols, 0)), axis=1), 0.0)


def _mla_weights(w_uq, w_qr, w_uk, w_uv):
    zq = jnp.zeros((MLA_Q_LORA, 32), F32)
    zk = jnp.zeros((MLA_KV_LORA, 64), F32)
    wq = jnp.concatenate([blk for h in range(MLA_HEADS)
                          for blk in (w_uq[:, 64 * h:64 * (h + 1)], w_qr[:, 32 * h:32 * (h + 1)], zq)], axis=1)
    wk = jnp.concatenate([blk for h in range(MLA_HEADS) for blk in (w_uk[:, 64 * h:64 * (h + 1)], zk)], axis=1)
    wv = jnp.concatenate([blk for h in range(MLA_HEADS)
                          for blk in ((zk, w_uv[:, 64 * h:64 * (h + 1)]) if h % 2 else
                                      (w_uv[:, 64 * h:64 * (h + 1)], zk))], axis=1)
    return wq.astype(BF16), jnp.concatenate([wk, wv], axis=1).astype(BF16)


def _slot_plan(route, counts, n_tok):
    n_slots = 2 * n_tok + N_EXPERTS * EB
    nblk = n_slots // EB
    cnt = counts.astype(jnp.int32)
    padded = (cnt + EB - 1) // EB * EB
    pad_ends = jnp.cumsum(padded)
    pad_starts = pad_ends - padded
    cols = route[:, 0:SUBLANES].T.astype(jnp.int32)
    dest = jnp.concatenate([pad_starts[cols[0]] + cols[4], pad_starts[cols[1]] + cols[5]])
    slot = jnp.arange(n_slots, dtype=jnp.int32)
    discard = 2 * n_tok + ((slot // EB) % 2) * EB + slot % EB
    res_row = discard.at[dest].set(jnp.arange(2 * n_tok, dtype=jnp.int32))
    src_row = res_row - n_tok * ((res_row >= n_tok).astype(jnp.int32) + (res_row >= 2 * n_tok).astype(jnp.int32))
    offs = jnp.concatenate([src_row.reshape(nblk, EB), res_row.reshape(nblk, EB)], axis=1) * TILE_ROWS
    block_start = jnp.arange(nblk, dtype=jnp.int32) * EB
    block_e = jnp.minimum(jnp.sum((pad_ends[None, :] <= block_start[:, None]).astype(jnp.int32), axis=1),
                          N_EXPERTS - 1)
    return block_e, offs


def kernel(x, c, ctx, c_ctx, w_ada, b_ada, w_in, w_out, conv_w, mla_q_norm, mla_kv_norm, mla_w_uq, mla_w_qr,
           mla_w_uk, mla_w_uv, diff_lambda, diff_subln, gqa_q_norm, gqa_k_norm, ln1_g, ln1_b, ln2_g, ln2_b,
           moe_w_group, moe_b_group, moe_w_sub, moe_b_sub, moe_w1, moe_w3, moe_w2):
    b, s, d = x.shape
    n_ctx = ctx.shape[1]
    depth = w_in.shape[0]
    assert n_ctx % TM == 0 and s % TM == 0 and s % GRID_W == 0 and d == SUBLANES * LANES
    nct = n_ctx // TM
    t = n_ctx + s
    alpha = (2 * depth) ** 0.25

    rows = ((b + 1 + SUBLANES - 1) // SUBLANES) * SUBLANES
    cc = jnp.zeros((rows, d), F32).at[:b].set(c).at[b].set(c_ctx)
    mod = _modulation(cc, w_ada, b_ada)
    tabs = _rope_tables(s, n_ctx)
    cols = _in_proj_columns()
    stream = (ctx, x, 0)

    for l in range(depth):
        last = l == depth - 1
        joff = nct if last else 0
        lam_init = 0.8 - 0.6 * math.exp(-0.3 * l)
        ml = mod[l].reshape(rows, 6, d)
        modsel = jnp.stack([jnp.broadcast_to(ml[b], (b, 6, d)), ml[:b]], axis=1)
        win_p = _relayout_columns(w_in[l], cols).astype(BF16)
        wq, wkv = _mla_weights(mla_w_uq[l], mla_w_qr[l], mla_w_uk[l], mla_w_uv[l])
        gq = jnp.tile(gqa_q_norm[l], 2)[None, :]
        gk = jnp.tile(gqa_k_norm[l], 2)[None, :]
        gb, zc, qm, km, vm, qd, kd, vd, qg, kg, vg = _proj(
            stream, t, modsel, win_p, tabs, mla_q_norm[l][None, :], mla_kv_norm[l][None, :], wq, wkv, gq, gk, nct)

        ym = _attention(qm, km, vm, (), _mla_heads, joff, nct, "attn_mla")
        yd = _attention(qd, kd, vd, (diff_lambda[l], jnp.tile(diff_subln[l], 2)[None, :]),
                        functools.partial(_diff_heads, lam_init), joff, nct, "attn_diff")
        yg = _attention(qg, kg, vg, (), _gqa_heads, joff, nct, "attn_gqa")

        wr = jnp.zeros((d, LANES), F32).at[:, :N_EXPERTS].set(moe_w_sub[l])
        wr = wr.at[:, N_EXPERTS:N_EXPERTS + N_GROUPS].set(moe_w_group[l])
        wr_hi = wr.astype(BF16)
        wr = jnp.concatenate([wr_hi, (wr - wr_hi.astype(F32)).astype(BF16)], axis=1)
        br = jnp.zeros((1, LANES), F32).at[0, :N_EXPERTS].set(moe_b_sub[l])
        br = br.at[0, N_EXPERTS:N_EXPERTS + N_GROUPS].set(moe_b_group[l])
        x1, tok, route, counts = _post(stream, t, modsel, gb, zc, ym, yd, yg, conv_w[l], w_out[l].astype(BF16),
                                       ln1_g[l][None, :], ln1_b[l][None, :], wr, br, joff, nct, alpha)

        n_tok = x1.shape[0] * x1.shape[1]
        block_e, offs = _slot_plan(route.reshape(n_tok, LANES), counts[0, :N_EXPERTS], n_tok)
        ys = _experts(block_e, offs, tok, moe_w1, moe_w3, moe_w2, l)
        xnext = _combine(x1, modsel, route, ys, ln2_g[l][None, :], ln2_b[l][None, :], joff, nct, alpha)
        stream = (xnext, xnext, nct)
    return xnext
```

```python
import functools
import math

import jax
import jax.numpy as jnp
import numpy as np
from jax import lax
from jax.experimental import pallas as pl
from jax.experimental.pallas import tpu as pltpu

F32 = jnp.float32
BF16 = jnp.bfloat16

GRID_W = 64
ROPE_THETA = 10000.0
NORM_EPS = 1e-6
CONV_CH = 256
MLA_HEADS, MLA_Q_LORA, MLA_KV_LORA, MLA_NOPE, MLA_ROPE, MLA_V = 4, 256, 128, 64, 32, 64
DIFF_HEADS, DIFF_QK, DIFF_V = 4, 32, 64
GQA_Q_HEADS, GQA_KV_HEADS, GQA_HD = 4, 2, 64
IN_CONV = 3 * CONV_CH
IN_MLA = MLA_Q_LORA + MLA_KV_LORA + MLA_ROPE
DIFF_QW = DIFF_HEADS * 2 * DIFF_QK
IN_DIFF = 2 * DIFF_QW + DIFF_HEADS * DIFF_V
GQA_QW = GQA_Q_HEADS * GQA_HD
GQA_KW = GQA_KV_HEADS * GQA_HD
OFF_MLA = IN_CONV
OFF_DIFF = OFF_MLA + IN_MLA
OFF_GQA = OFF_DIFF + IN_DIFF
MLA_SCALE = (MLA_NOPE + MLA_ROPE) ** -0.5
DIFF_SCALE = DIFF_QK ** -0.5
GQA_SCALE = GQA_HD ** -0.5
N_GROUPS, EXPERTS_PER_GROUP = 4, 8
N_EXPERTS = N_GROUPS * EXPERTS_PER_GROUP
D_EXPERT = 256
LOG2E = math.log2(math.e)

LANES = 128
SUBLANES = 8
TM = 256
EB = 256
VMEM_LIMIT = 56 * 1024 * 1024

P_CONV = 0
P_MLA = P_CONV + 768
P_DIFF = P_MLA + 512
P_GQA = P_DIFF + 1024
P_END = P_GQA + 896


def _cparams(sem, flags=None):
    return pltpu.CompilerParams(dimension_semantics=sem, vmem_limit_bytes=VMEM_LIMIT, flags=flags)


def _lane(shape):
    return lax.broadcasted_iota(jnp.int32, shape, len(shape) - 1)


def _mod_kernel(c_ref, w_ref, b_ref, o_ref):
    c = c_ref[...]
    sc = (c * jax.nn.sigmoid(c)).astype(BF16)
    o_ref[0] = jnp.dot(sc, w_ref[0].astype(BF16), preferred_element_type=F32) + b_ref[0]


def _modulation(cc, w_ada, b_ada):
    n_layers, d, d6 = w_ada.shape
    r = cc.shape[0]
    return pl.pallas_call(
        _mod_kernel,
        grid=(n_layers, d6 // d),
        in_specs=[pl.BlockSpec((r, d), lambda l, n: (0, 0)),
                  pl.BlockSpec((1, d, d), lambda l, n: (l, 0, n)),
                  pl.BlockSpec((1, 1, d), lambda l, n: (l, 0, n))],
        out_specs=pl.BlockSpec((1, r, d), lambda l, n: (l, 0, n)),
        out_shape=jax.ShapeDtypeStruct((n_layers, r, d6), F32),
        compiler_params=_cparams(("arbitrary", "arbitrary")),
        name="mod",
    )(cc, w_ada, b_ada.reshape(n_layers, 1, d6))


def _rope(z, cos, sin_signed, half):
    n = z.shape[-1]
    first = (_lane(z.shape) % (2 * half)) < half
    partner = jnp.where(first, pltpu.roll(z, n - half, 1), pltpu.roll(z, half, 1))
    return z * cos + partner * sin_signed


def _rms(z, gain):
    return z * lax.rsqrt(jnp.mean(z * z, axis=-1, keepdims=True) + NORM_EPS) * gain


def _seg_rms64(z, gain):
    lo = _lane(z.shape) < 64
    z2 = z * z
    s_lo = jnp.sum(jnp.where(lo, z2, 0.0), axis=-1, keepdims=True)
    s_hi = jnp.sum(jnp.where(lo, 0.0, z2), axis=-1, keepdims=True)
    ms = jnp.where(lo, s_lo, s_hi) * (1.0 / 64)
    return z * lax.rsqrt(ms + NORM_EPS) * gain


def _ones_lane(odd):
    return (_lane((1, LANES)) == (0 if odd else 64)).astype(F32)


def _samples_per_step(batch):
    return 2 if batch % 2 == 0 else 1


def _stream_tile(xa_ref, xb_ref, jj, nct):
    return jnp.where(jj < nct, xa_ref[...], xb_ref[...])


def _stream_specs(stream, tb, joff, nct):
    xa, xb, lat_off = stream
    d = xa.shape[2]
    return [pl.BlockSpec((tb, TM, d), lambda bi, j: (bi, jnp.minimum(j + joff, nct - 1), 0)),
            pl.BlockSpec((tb, TM, d), lambda bi, j: (bi, jnp.maximum(j + joff - nct, 0) + lat_off, 0))]


def _put_rows(ref, lanes, val):
    for u in range(ref.shape[0]):
        ref[u, :, lanes] = val[u * TM:(u + 1) * TM]


def _proj_kernel(xa_ref, xb_ref, mod_ref, win_ref, cm_ref, sm_ref, cd_ref, sd_ref, cg_ref, sg_ref,
                 qn_ref, kvn_ref, wq_ref, wkv_ref, gq_ref, gk_ref,
                 gb_ref, zc_ref, qm_ref, km_ref, vm_ref, qd_ref, kd_ref, vd_ref, qg_ref, kg_ref, vg_ref, *, nct):
    x = _stream_tile(xa_ref, xb_ref, pl.program_id(1), nct)
    tb, _, d = x.shape
    sh1 = mod_ref[:, 0, 0:1, :]
    sc1 = mod_ref[:, 0, 1:2, :]
    xm = (x * (1.0 + sc1) + sh1).reshape(tb * TM, d).astype(BF16)
    everything = slice(None)

    def table(ref):
        return jnp.concatenate([ref[...]] * tb, axis=0)

    uc = jnp.dot(xm, win_ref[:, P_CONV:P_MLA], preferred_element_type=F32)
    _put_rows(gb_ref, everything, uc[:, 0:256])
    _put_rows(zc_ref, everything, uc[:, 256:512] * uc[:, 512:768])

    um = jnp.dot(xm, win_ref[:, P_MLA:P_DIFF], preferred_element_type=F32)
    cm, sm = table(cm_ref), table(sm_ref)
    cq = _rms(um[:, 0:256], qn_ref[...]).astype(BF16)
    qall = jnp.dot(cq, wq_ref[...], preferred_element_type=F32)
    ckv = _rms(um[:, 256:384], kvn_ref[...]).astype(BF16)
    kv = jnp.dot(ckv, wkv_ref[...], preferred_element_type=F32)
    kr = _rope(um[:, 384:512], cm, sm, 8)
    for h in range(MLA_HEADS):
        sl = slice(LANES * h, LANES * (h + 1))
        _put_rows(qm_ref, sl, (_rope(qall[:, sl], cm, sm, 8) * (MLA_SCALE * LOG2E)).astype(BF16))
        _put_rows(km_ref, sl, (kv[:, sl] + kr).astype(BF16))
        _put_rows(vm_ref, sl, (kv[:, 512 + LANES * h:512 + LANES * (h + 1)] + _ones_lane(h % 2)).astype(BF16))

    ud = jnp.dot(xm, win_ref[:, P_DIFF:P_GQA], preferred_element_type=F32)
    cd, sd = table(cd_ref), table(sd_ref)
    for c in range(2):
        sl = slice(LANES * c, LANES * (c + 1))
        _put_rows(qd_ref, sl, (_rope(ud[:, sl], cd, sd, 8) * (DIFF_SCALE * LOG2E)).astype(BF16))
        _put_rows(kd_ref, sl, _rope(ud[:, 256 + LANES * c:256 + LANES * (c + 1)], cd, sd, 8).astype(BF16))
    for h in range(DIFF_HEADS):
        sl = slice(LANES * h, LANES * (h + 1))
        _put_rows(vd_ref, sl, (ud[:, 512 + LANES * h:512 + LANES * (h + 1)] + _ones_lane(h % 2)).astype(BF16))

    ug = jnp.dot(xm, win_ref[:, P_GQA:P_END], preferred_element_type=F32)
    cg, sg = table(cg_ref), table(sg_ref)
    for c in range(2):
        sl = slice(LANES * c, LANES * (c + 1))
        _put_rows(qg_ref, sl,
                  (_rope(_seg_rms64(ug[:, sl], gq_ref[...]), cg, sg, 16) * (GQA_SCALE * LOG2E)).astype(BF16))
    _put_rows(kg_ref, everything, _rope(_seg_rms64(ug[:, 256:384], gk_ref[...]), cg, sg, 16).astype(BF16))
    for i in range(4):
        sl = slice(LANES * i, LANES * (i + 1))
        _put_rows(vg_ref, sl, (ug[:, 384 + LANES * i:384 + LANES * (i + 1)] + _ones_lane(i % 2)).astype(BF16))


def _proj(stream, t, modsel, win_p, tabs, qn, kvn, wq, wkv, gq, gk, nct):
    b, _, d = stream[0].shape
    nt = t // TM
    tb = 1

    def tile(width):
        return pl.BlockSpec((tb, TM, width), lambda bi, j: (bi, j, 0))

    def full(arr):
        return pl.BlockSpec(arr.shape, lambda bi, j: (0,) * arr.ndim)

    tab_spec = pl.BlockSpec((TM, LANES), lambda bi, j: (j, 0))
    widths = [(256, F32), (256, F32), (512, BF16), (512, BF16), (512, BF16), (256, BF16), (256, BF16),
              (512, BF16), (256, BF16), (128, BF16), (512, BF16)]
    return pl.pallas_call(
        functools.partial(_proj_kernel, nct=nct),
        grid=(b // tb, nt),
        in_specs=_stream_specs(stream, tb, 0, nct) + [
                  pl.BlockSpec((tb, 1, 6, d), lambda bi, j: (bi, jnp.minimum(j // nct, 1), 0, 0)),
                  full(win_p)] + [tab_spec] * 6 + [full(a) for a in (qn, kvn, wq, wkv, gq, gk)],
        out_specs=[tile(w) for w, _ in widths],
        out_shape=[jax.ShapeDtypeStruct((b, t, w), dt) for w, dt in widths],
        compiler_params=_cparams(("arbitrary", "arbitrary")),
        name="proj",
    )(stream[0], stream[1], modsel, win_p, *tabs, qn, kvn, wq, wkv, gq, gk)


def _scores(q_ref, k_ref, tk, chunk, mask):
    q = q_ref[0, :, LANES * chunk:LANES * (chunk + 1)]
    if mask is not None:
        lane = _lane(q.shape)
        q = jnp.where((lane >= mask[0]) & (lane < mask[1]), q, jnp.zeros_like(q))
    k = k_ref[0, 0:tk, LANES * chunk:LANES * (chunk + 1)]
    return lax.dot_general(q, k, (((1,), (1,)), ((), ())), preferred_element_type=F32)


def _attend(s, v, odd):
    p = jnp.exp2(s - jnp.max(s, axis=-1, keepdims=True)).astype(BF16)
    o = jnp.dot(p, v, preferred_element_type=F32)
    den = o[:, 0:1] if odd else o[:, 64:65]
    return o * (1.0 / den)


def _pair(even, odd):
    return jnp.where(_lane(even.shape) < 64, even, odd)


def _vblock(v_ref, tk, idx):
    return v_ref[0, 0:tk, LANES * idx:LANES * (idx + 1)]


def _one_ahead(n, score_fn, finish_fn):
    outs, nxt = [], score_fn(0)
    for h in range(n):
        cur = nxt
        if h + 1 < n:
            nxt = score_fn(h + 1)
        outs.append(finish_fn(h, cur))
    return outs


def _mla_heads(q_ref, k_ref, v_ref, tk, extra):
    o = _one_ahead(MLA_HEADS, lambda h: _scores(q_ref, k_ref, tk, h, None),
                   lambda h, s: _attend(s, _vblock(v_ref, tk, h), h % 2))
    return jnp.concatenate([_pair(o[0], o[1]), _pair(o[2], o[3])], axis=-1)


def _gqa_heads(q_ref, k_ref, v_ref, tk, extra):
    def scores(i):
        kvh, g = divmod(i, GQA_Q_HEADS // GQA_KV_HEADS)
        q = q_ref[0, :, LANES * g:LANES * (g + 1)]
        lane = _lane(q.shape)
        q = jnp.where((lane >= 64 * kvh) & (lane < 64 * kvh + 64), q, jnp.zeros_like(q))
        return lax.dot_general(q, k_ref[0, 0:tk, :], (((1,), (1,)), ((), ())), preferred_element_type=F32)

    o = _one_ahead(GQA_Q_HEADS, scores, lambda i, s: _attend(s, _vblock(v_ref, tk, i), i % 2))
    return jnp.concatenate([_pair(o[0], o[1]), _pair(o[2], o[3])], axis=-1)


def _diff_heads(lam_init, q_ref, k_ref, v_ref, tk, extra):
    dl_ref, subln_ref = extra
    dl = dl_ref[...]
    lam = (jnp.exp(jnp.sum(dl[0:1] * dl[1:2], axis=-1, keepdims=True))
           - jnp.exp(jnp.sum(dl[2:3] * dl[3:4], axis=-1, keepdims=True)) + lam_init)

    def head_scores(h):
        return [_scores(q_ref, k_ref, tk, (2 * h + m) // 4, (32 * ((2 * h + m) % 4), 32 * ((2 * h + m) % 4) + 32))
                for m in range(2)]

    def finish(h, pair):
        probs = []
        for s in pair:
            e = jnp.exp2(s - jnp.max(s, axis=-1, keepdims=True))
            probs.append((e, jnp.sum(e, axis=-1, keepdims=True)))
        a = probs[0][0] * (1.0 / probs[0][1]) - probs[1][0] * (lam / probs[1][1])
        return jnp.dot(a.astype(BF16), _vblock(v_ref, tk, h), preferred_element_type=F32)

    d = _one_ahead(DIFF_HEADS, head_scores, finish)
    chunks = [_seg_rms64(_pair(d[2 * c], d[2 * c + 1]), subln_ref[...]) * (1.0 - lam_init) for c in range(2)]
    return jnp.concatenate(chunks, axis=-1)


def _attn_kernel(*refs, heads, joff, nct, ctx_len, total_len):
    q_ref, k_ref, v_ref = refs[:3]
    extra, o_ref = refs[3:-1], refs[-1]
    jj = pl.program_id(1) + joff

    def run(tk):
        o_ref[0] = heads(q_ref, k_ref, v_ref, tk, extra).astype(o_ref.dtype)

    if joff < nct:
        pl.when(jj < nct)(lambda: run(ctx_len))
        pl.when(jj >= nct)(lambda: run(total_len))
    else:
        run(total_len)


def _attention(q, k, v, extra, heads, joff, nct, name):
    b, t, _ = q.shape
    nt = t // TM

    def full(arr):
        return pl.BlockSpec(arr.shape, lambda bi, j: (0,) * arr.ndim)

    kern = functools.partial(_attn_kernel, heads=heads, joff=joff, nct=nct, ctx_len=nct * TM, total_len=t)
    return pl.pallas_call(
        kern,
        grid=(b, nt - joff),
        in_specs=[pl.BlockSpec((1, TM, q.shape[2]), lambda bi, j: (bi, j + joff, 0)),
                  pl.BlockSpec((1, t, k.shape[2]), lambda bi, j: (bi, 0, 0)),
                  pl.BlockSpec((1, t, v.shape[2]), lambda bi, j: (bi, 0, 0))] + [full(a) for a in extra],
        out_specs=pl.BlockSpec((1, TM, 256), lambda bi, j: (bi, j, 0)),
        out_shape=jax.ShapeDtypeStruct((b, t - joff * TM, 256), BF16),
        compiler_params=_cparams(("arbitrary", "arbitrary")),
        name=name,
    )(q, k, v, *extra)


def _layer_norm(h, g, bias):
    mu = jnp.mean(h, axis=-1, keepdims=True)
    hc = h - mu
    var = jnp.mean(hc * hc, axis=-1, keepdims=True)
    return hc * lax.rsqrt(var + NORM_EPS) * g + bias


def _store_token_tiles(ref, val):
    rows = val.shape[0]
    for c in range(SUBLANES):
        ref[pl.ds(c, rows, stride=SUBLANES), :] = val[:, LANES * c:LANES * (c + 1)]


def _load_token_tiles(ref, rows):
    return jnp.concatenate([ref[pl.ds(c, rows, stride=SUBLANES), :] for c in range(SUBLANES)], axis=1)


def _min_lane(cond, lane_f):
    return jnp.min(jnp.where(cond, lane_f, float(LANES)), axis=-1, keepdims=True)


def _post_kernel(xa_ref, xb_ref, mod_ref, gb_ref, zc_ref, zp_ref, zn_ref, ym_ref, yd_ref, yg_ref, cw_ref, wout_ref,
                 g_ref, b_ref, wr_ref, br_ref, x1_ref, tok_ref, route_ref, cnt_ref, cnt_acc,
                 *, joff, nct, nt, alpha):
    jj = pl.program_id(1) + joff
    first_step = (pl.program_id(0) == 0) & (pl.program_id(1) == 0)

    @pl.when(first_step)
    def _():
        cnt_acc[...] = jnp.zeros_like(cnt_acc)

    tb = zc_ref.shape[0]
    rows = tb * TM
    row = lax.broadcasted_iota(jnp.int32, (TM, zc_ref.shape[2]), 0)
    left_ok = (jj != 0) & (jj != nct)
    right_ok = (jj != nct - 1) & (jj != nt - 1)
    cw = cw_ref[...]
    mixed = []
    for u in range(tb):
        zc = zc_ref[u]
        halo_prev = jnp.where(left_ok, zp_ref[u, SUBLANES - 1:SUBLANES, :], 0.0)
        halo_next = jnp.where(right_ok, zn_ref[u, 0:1, :], 0.0)
        zprev = jnp.where(row == 0, halo_prev, pltpu.roll(zc, 1, 0))
        znext = jnp.where(row == TM - 1, halo_next, pltpu.roll(zc, TM - 1, 0))
        conv = zprev * cw[0:1] + zc * cw[1:2] + znext * cw[2:3]
        mixed.append(jnp.concatenate([(gb_ref[u] * conv).astype(BF16), ym_ref[u], yd_ref[u], yg_ref[u]], axis=-1))
    y = jnp.concatenate(mixed, axis=0)
    acc = jnp.dot(y, wout_ref[...], preferred_element_type=F32)
    d = acc.shape[1]

    g1 = mod_ref[:, 0, 2:3, :]
    sh2 = mod_ref[:, 0, 3:4, :]
    sc2 = mod_ref[:, 0, 4:5, :]
    x1 = _layer_norm(alpha * _stream_tile(xa_ref, xb_ref, jj, nct) + g1 * acc.reshape(tb, TM, d),
                     g_ref[...], b_ref[...])
    x1_ref[...] = x1
    tok3 = x1 * (1.0 + sc2) + sh2
    for u in range(tb):
        _store_token_tiles(tok_ref.at[u], tok3[u])
    tok = tok3.reshape(rows, d)

    tok_hi = tok.astype(BF16)
    tok_lo = (tok - tok_hi.astype(F32)).astype(BF16)
    part = jnp.dot(tok_hi, wr_ref[...], preferred_element_type=F32)
    logits = (part[:, 0:LANES] + (part[:, LANES:2 * LANES]
                                  + jnp.dot(tok_lo, wr_ref[:, 0:LANES], preferred_element_type=F32))) + br_ref[...]
    lane = _lane(logits.shape)
    lane_f = lane.astype(F32)
    neg = -jnp.inf
    is_g = (lane >= N_EXPERTS) & (lane < N_EXPERTS + N_GROUPS)
    lg = jnp.where(is_g, logits, neg)
    eg = jnp.exp(lg - jnp.max(lg, axis=-1, keepdims=True))
    pg = eg / jnp.sum(eg, axis=-1, keepdims=True)
    g_w = jnp.max(pg, axis=-1, keepdims=True)
    g_lane = _min_lane(is_g & (pg == g_w), lane_f)
    g_idx = g_lane - float(N_EXPERTS)
    in_grp = (lane < N_EXPERTS) & ((lane // EXPERTS_PER_GROUP).astype(F32) == g_idx)
    ls = jnp.where(in_grp, logits, neg)
    es = jnp.exp(ls - jnp.max(ls, axis=-1, keepdims=True))
    ps = es / jnp.sum(es, axis=-1, keepdims=True)
    p1 = jnp.max(jnp.where(in_grp, ps, -1.0), axis=-1, keepdims=True)
    i1 = _min_lane(in_grp & (ps == p1), lane_f)
    rest = in_grp & (lane_f != i1)
    p2 = jnp.max(jnp.where(rest, ps, -1.0), axis=-1, keepdims=True)
    i2 = _min_lane(rest & (ps == p2), lane_f)
    tot = p1 + p2
    w1 = g_w * (p1 / tot)
    w2 = g_w * (p2 / tot)

    oh1 = lane_f == i1
    oh2 = lane_f == i2
    r_i = lax.broadcasted_iota(jnp.int32, (rows, rows), 0)
    c_i = lax.broadcasted_iota(jnp.int32, (rows, rows), 1)
    tri = (r_i > c_i).astype(BF16)
    before1 = jnp.dot(tri, oh1.astype(BF16), preferred_element_type=F32)
    before2 = jnp.dot(tri, oh2.astype(BF16), preferred_element_type=F32)
    tot1 = jnp.sum(oh1.astype(F32), axis=0, keepdims=True)
    tot2 = jnp.sum(oh2.astype(F32), axis=0, keepdims=True)
    base = cnt_acc[...]
    r1 = jnp.sum(jnp.where(oh1, base + before1, 0.0), axis=-1, keepdims=True)
    r2 = jnp.sum(jnp.where(oh2, base + tot1 + before2, 0.0), axis=-1, keepdims=True)
    new_cnt = base + tot1 + tot2
    cnt_acc[...] = new_cnt
    cnt_ref[...] = jnp.broadcast_to(new_cnt, cnt_ref.shape)

    vals = (i1, i2, w1, w2, r1, r2)
    route = jnp.zeros(logits.shape, F32)
    for idx, val in enumerate(vals):
        route = jnp.where(lane == idx, val, route)
    _put_rows(route_ref, slice(None), route)


def _post(stream, t, modsel, gb, zc, ym, yd, yg, conv_w, wout, ln_g, ln_b, wr, br, joff, nct, alpha):
    b, _, d = stream[0].shape
    nt = t // TM
    nj = nt - joff
    hb = TM // SUBLANES
    tb = _samples_per_step(b)

    def tile(width):
        return pl.BlockSpec((tb, TM, width), lambda bi, j: (bi, j + joff, 0))

    def otile(width):
        return pl.BlockSpec((tb, TM, width), lambda bi, j: (bi, j, 0))

    def full(arr):
        return pl.BlockSpec(arr.shape, lambda bi, j: (0,) * arr.ndim)

    kern = functools.partial(_post_kernel, joff=joff, nct=nct, nt=nt, alpha=alpha)
    return pl.pallas_call(
        kern,
        grid=(b // tb, nj),
        in_specs=_stream_specs(stream, tb, joff, nct) + [
                  pl.BlockSpec((tb, 1, 6, d), lambda bi, j: (bi, jnp.minimum((j + joff) // nct, 1), 0, 0)),
                  tile(256), tile(256),
                  pl.BlockSpec((tb, SUBLANES, 256), lambda bi, j: (bi, jnp.maximum((j + joff) * hb - 1, 0), 0)),
                  pl.BlockSpec((tb, SUBLANES, 256),
                               lambda bi, j: (bi, jnp.minimum((j + joff + 1) * hb, nt * hb - 1), 0)),
                  otile(256), otile(256), otile(256),
                  full(conv_w), full(wout), full(ln_g), full(ln_b), full(wr), full(br)],
        out_specs=[otile(d), pl.BlockSpec((tb, TM * SUBLANES, LANES), lambda bi, j: (bi, j, 0)),
                   otile(LANES), pl.BlockSpec((SUBLANES, LANES), lambda bi, j: (0, 0))],
        out_shape=[jax.ShapeDtypeStruct((b, nj * TM, d), F32),
                   jax.ShapeDtypeStruct((b, nj * TM * SUBLANES, LANES), F32),
                   jax.ShapeDtypeStruct((b, nj * TM, LANES), F32),
                   jax.ShapeDtypeStruct((SUBLANES, LANES), F32)],
        scratch_shapes=[pltpu.VMEM((1, LANES), F32)],
        compiler_params=_cparams(("arbitrary", "arbitrary")),
        name="post",
    )(stream[0], stream[1], modsel, gb, zc, zc, zc, ym, yd, yg, conv_w, wout, ln_g, ln_b, wr, br)


ROW_UNROLL = EB
TILE_ROWS = SUBLANES


def _expert_kernel(be_ref, offs_ref, tok_ref, w1_ref, w3_ref, w2_ref, out_ref, xbuf, ybuf, gsem, ssem,
                   *, n_tok, nblk):
    del be_ref
    i = pl.program_id(0)
    slot = i % 2
    blk_rows = EB * TILE_ROWS

    def tile_at(ref, first_row):
        return ref.at[pl.ds(pl.multiple_of(first_row, TILE_ROWS), TILE_ROWS)]

    def gather_start(blk, sl):
        def body(g, carry):
            for u in range(ROW_UNROLL):
                r = g * ROW_UNROLL + u
                pltpu.make_async_copy(tile_at(tok_ref, offs_ref[blk, r]), tile_at(xbuf.at[sl], r * TILE_ROWS),
                                      gsem.at[sl]).start(priority=u % 2)
            return carry
        lax.fori_loop(0, EB // ROW_UNROLL, body, 0)

    def scatter_start(blk, sl):
        def body(g, carry):
            for u in range(ROW_UNROLL):
                r = g * ROW_UNROLL + u
                pltpu.make_async_copy(tile_at(ybuf.at[sl], r * TILE_ROWS), tile_at(out_ref, offs_ref[blk, EB + r]),
                                      ssem.at[sl]).start(priority=u % 2)
            return carry
        lax.fori_loop(0, EB // ROW_UNROLL, body, 0)

    def gather_wait(sl):
        pltpu.make_async_copy(tok_ref.at[pl.ds(0, blk_rows)], xbuf.at[sl], gsem.at[sl]).wait()

    def scatter_wait(sl):
        pltpu.make_async_copy(ybuf.at[sl], out_ref.at[pl.ds(0, blk_rows)], ssem.at[sl]).wait()

    @pl.when(i == 0)
    def _():
        gather_start(0, 0)

    @pl.when(i + 1 < nblk)
    def _():
        gather_start(i + 1, 1 - slot)

    gather_wait(slot)

    @pl.when(i >= 2)
    def _():
        scatter_wait(slot)

    xb = _load_token_tiles(xbuf.at[slot], EB).astype(BF16)
    h1 = jnp.dot(xb, w1_ref[0, 0].astype(BF16), preferred_element_type=F32)
    h3 = jnp.dot(xb, w3_ref[0, 0].astype(BF16), preferred_element_type=F32)
    a = (h1 * jax.nn.sigmoid(h1) * h3).astype(BF16)
    _store_token_tiles(ybuf.at[slot], jnp.dot(a, w2_ref[0, 0].astype(BF16), preferred_element_type=F32))
    scatter_start(i, slot)

    @pl.when(i == nblk - 1)
    def _():
        scatter_wait(1 - slot)
        scatter_wait(slot)
        ybuf[0] = jnp.zeros(ybuf.shape[1:], F32)
        for half in range(2):
            cp = pltpu.make_async_copy(
                ybuf.at[0], out_ref.at[pl.ds((2 * n_tok + half * EB) * TILE_ROWS, blk_rows)], ssem.at[0])
            cp.start()
            cp.wait()


def _experts(block_e, offs, tok_tiles, w1, w3, w2, layer):
    n_tok = tok_tiles.shape[0] // TILE_ROWS
    d, de = w1.shape[2], w1.shape[3]
    nblk = offs.shape[0]
    assert nblk >= 2 and d == TILE_ROWS * LANES
    kern = functools.partial(_expert_kernel, n_tok=n_tok, nblk=nblk)
    return pl.pallas_call(
        kern,
        grid_spec=pltpu.PrefetchScalarGridSpec(
            num_scalar_prefetch=2,
            grid=(nblk,),
            in_specs=[pl.BlockSpec(memory_space=pl.ANY),
                      pl.BlockSpec((1, 1, d, de), lambda i, be, offs: (layer, be[i], 0, 0)),
                      pl.BlockSpec((1, 1, d, de), lambda i, be, offs: (layer, be[i], 0, 0)),
                      pl.BlockSpec((1, 1, de, d), lambda i, be, offs: (layer, be[i], 0, 0))],
            out_specs=pl.BlockSpec(memory_space=pl.ANY),
            scratch_shapes=[pltpu.VMEM((2, EB * TILE_ROWS, LANES), F32), pltpu.VMEM((2, EB * TILE_ROWS, LANES), F32),
                            pltpu.SemaphoreType.DMA((2,)), pltpu.SemaphoreType.DMA((2,))]),
        out_shape=jax.ShapeDtypeStruct(((2 * n_tok + 2 * EB) * TILE_ROWS, LANES), F32),
        compiler_params=_cparams(("arbitrary",)),
        name="experts",
    )(block_e, offs, tok_tiles, w1, w3, w2)


def _combine_kernel(x1_ref, mod_ref, route_ref, f0_ref, f1_ref, g_ref, b_ref, o_ref, *, alpha):
    route = route_ref[0]
    f = route[:, 2:3] * _load_token_tiles(f0_ref, TM) + route[:, 3:4] * _load_token_tiles(f1_ref, TM)
    g2 = mod_ref[0, 0, 5:6, :]
    o_ref[0] = _layer_norm(alpha * x1_ref[0] + g2 * f, g_ref[...], b_ref[...])


def _combine(x1, modsel, route, ys, ln_g, ln_b, joff, nct, alpha):
    b, n, d = x1.shape
    nj = n // TM
    plane = b * nj

    def full(arr):
        return pl.BlockSpec(arr.shape, lambda bi, j: (0,) * arr.ndim)

    return pl.pallas_call(
        functools.partial(_combine_kernel, alpha=alpha),
        grid=(b, nj),
        in_specs=[pl.BlockSpec((1, TM, d), lambda bi, j: (bi, j, 0)),
                  pl.BlockSpec((1, 1, 6, d), lambda bi, j: (bi, jnp.minimum((j + joff) // nct, 1), 0, 0)),
                  pl.BlockSpec((1, TM, LANES), lambda bi, j: (bi, j, 0)),
                  pl.BlockSpec((TM * TILE_ROWS, LANES), lambda bi, j: (bi * nj + j, 0)),
                  pl.BlockSpec((TM * TILE_ROWS, LANES), lambda bi, j: (plane + bi * nj + j, 0)),
                  full(ln_g), full(ln_b)],
        out_specs=pl.BlockSpec((1, TM, d), lambda bi, j: (bi, j, 0)),
        out_shape=jax.ShapeDtypeStruct((b, n, d), F32),
        compiler_params=_cparams(("arbitrary", "arbitrary")),
        name="combine",
    )(x1, modsel, route, ys, ys, ln_g, ln_b)


def _rope_tables(seq, ctx_len):
    t = jnp.arange(seq, dtype=jnp.int32)
    row = (t // GRID_W).astype(F32)
    col = (t % GRID_W).astype(F32)

    def table(vec_dim):
        half = vec_dim // 4
        freqs = ROPE_THETA ** (-jnp.arange(half, dtype=F32) / half)
        lane = np.arange(vec_dim)
        idx = lane % half
        use_col = (lane // (2 * half)) == 1
        ang = jnp.where(use_col[None, :], col[:, None], row[:, None]) * freqs[idx][None, :]
        sign = np.where((lane % (2 * half)) < half, -1.0, 1.0).astype(np.float32)
        return jnp.cos(ang), jnp.sin(ang) * sign[None, :]

    def with_ctx(cos, sin):
        width = cos.shape[1]
        return (jnp.concatenate([jnp.ones((ctx_len, width), F32), cos], axis=0),
                jnp.concatenate([jnp.zeros((ctx_len, width), F32), sin], axis=0))

    c32, s32 = table(32)
    c64, s64 = table(64)
    cd, sd = with_ctx(jnp.tile(c32, (1, 4)), jnp.tile(s32, (1, 4)))
    cg, sg = with_ctx(jnp.tile(c64, (1, 2)), jnp.tile(s64, (1, 2)))
    ones, zeros = jnp.ones((seq, 64), F32), jnp.zeros((seq, 64), F32)
    cm, sm = with_ctx(jnp.concatenate([ones, c32, ones[:, :32]], axis=1),
                      jnp.concatenate([zeros, s32, zeros[:, :32]], axis=1))
    return cm, sm, cd, sd, cg, sg


def _in_proj_columns():
    def spread(start, odd_blocks):
        cols = []
        for i, odd in enumerate(odd_blocks):
            vals = list(range(start + 64 * i, start + 64 * (i + 1)))
            cols += ([-1] * 64 + vals) if odd else (vals + [-1] * 64)
        return cols

    cols = list(range(0, IN_CONV))
    cols += list(range(OFF_MLA, OFF_MLA + MLA_Q_LORA + MLA_KV_LORA))
    cols += [-1] * 64 + list(range(OFF_MLA + MLA_Q_LORA + MLA_KV_LORA, OFF_DIFF)) + [-1] * 32
    cols += list(range(OFF_DIFF, OFF_DIFF + 2 * DIFF_QW))
    cols += spread(OFF_DIFF + 2 * DIFF_QW, [h % 2 for h in range(DIFF_HEADS)])
    q0 = OFF_GQA
    head = lambda h: list(range(q0 + GQA_HD * h, q0 + GQA_HD * (h + 1)))
    cols += head(0) + head(2) + head(1) + head(3)
    cols += list(range(OFF_GQA + GQA_QW, OFF_GQA + GQA_QW + GQA_KW))
    v0 = OFF_GQA + GQA_QW + GQA_KW
    for kvh in range(GQA_KV_HEADS):
        vals = list(range(v0 + GQA_HD * kvh, v0 + GQA_HD * (kvh + 1)))
        cols += vals + [-1] * 64 + [-1] * 64 + vals
    assert len(cols) == P_END
    return np.asarray(cols, np.int32)


def _relayout_columns(w, cols):
    valid = jnp.asarray(cols >= 0)
    return jnp.where(valid[None, :], jnp.take(w, jnp.asarray(np.maximum(cols, 0)), axis=1), 0.0)


def _mla_weights(w_uq, w_qr, w_uk, w_uv):
    zq = jnp.zeros((MLA_Q_LORA, 32), F32)
    zk = jnp.zeros((MLA_KV_LORA, 64), F32)
    wq = jnp.concatenate([blk for h in range(MLA_HEADS)
                          for blk in (w_uq[:, 64 * h:64 * (h + 1)], w_qr[:, 32 * h:32 * (h + 1)], zq)], axis=1)
    wk = jnp.concatenate([blk for h in range(MLA_HEADS) for blk in (w_uk[:, 64 * h:64 * (h + 1)], zk)], axis=1)
    wv = jnp.concatenate([blk for h in range(MLA_HEADS)
                          for blk in ((zk, w_uv[:, 64 * h:64 * (h + 1)]) if h % 2 else
                                      (w_uv[:, 64 * h:64 * (h + 1)], zk))], axis=1)
    return wq.astype(BF16), jnp.concatenate([wk, wv], axis=1).astype(BF16)


def _slot_plan(route, counts, n_tok):
    n_slots = 2 * n_tok + N_EXPERTS * EB
    nblk = n_slots // EB
    cnt = counts.astype(jnp.int32)
    padded = (cnt + EB - 1) // EB * EB
    pad_ends = jnp.cumsum(padded)
    pad_starts = pad_ends - padded
    cols = route[:, 0:SUBLANES].T.astype(jnp.int32)
    dest = jnp.concatenate([pad_starts[cols[0]] + cols[4], pad_starts[cols[1]] + cols[5]])
    slot = jnp.arange(n_slots, dtype=jnp.int32)
    discard = 2 * n_tok + ((slot // EB) % 2) * EB + slot % EB
    res_row = discard.at[dest].set(jnp.arange(2 * n_tok, dtype=jnp.int32))
    src_row = res_row - n_tok * ((res_row >= n_tok).astype(jnp.int32) + (res_row >= 2 * n_tok).astype(jnp.int32))
    offs = jnp.concatenate([src_row.reshape(nblk, EB), res_row.reshape(nblk, EB)], axis=1) * TILE_ROWS
    block_start = jnp.arange(nblk, dtype=jnp.int32) * EB
    block_e = jnp.minimum(jnp.sum((pad_ends[None, :] <= block_start[:, None]).astype(jnp.int32), axis=1),
                          N_EXPERTS - 1)
    return block_e, offs


def kernel(x, c, ctx, c_ctx, w_ada, b_ada, w_in, w_out, conv_w, mla_q_norm, mla_kv_norm, mla_w_uq, mla_w_qr,
           mla_w_uk, mla_w_uv, diff_lambda, diff_subln, gqa_q_norm, gqa_k_norm, ln1_g, ln1_b, ln2_g, ln2_b,
           moe_w_group, moe_b_group, moe_w_sub, moe_b_sub, moe_w1, moe_w3, moe_w2):
    b, s, d = x.shape
    n_ctx = ctx.shape[1]
    depth = w_in.shape[0]
    assert n_ctx % TM == 0 and s % TM == 0 and s % GRID_W == 0 and d == SUBLANES * LANES
    nct = n_ctx // TM
    t = n_ctx + s
    alpha = (2 * depth) ** 0.25

    rows = ((b + 1 + SUBLANES - 1) // SUBLANES) * SUBLANES
    cc = jnp.zeros((rows, d), F32).at[:b].set(c).at[b].set(c_ctx)
    mod = _modulation(cc, w_ada, b_ada)
    tabs = _rope_tables(s, n_ctx)
    cols = _in_proj_columns()
    stream = (ctx, x, 0)

    for l in range(depth):
        last = l == depth - 1
        joff = nct if last else 0
        lam_init = 0.8 - 0.6 * math.exp(-0.3 * l)
        ml = mod[l].reshape(rows, 6, d)
        modsel = jnp.stack([jnp.broadcast_to(ml[b], (b, 6, d)), ml[:b]], axis=1)
        win_p = _relayout_columns(w_in[l], cols).astype(BF16)
        wq, wkv = _mla_weights(mla_w_uq[l], mla_w_qr[l], mla_w_uk[l], mla_w_uv[l])
        gq = jnp.tile(gqa_q_norm[l], 2)[None, :]
        gk = jnp.tile(gqa_k_norm[l], 2)[None, :]
        gb, zc, qm, km, vm, qd, kd, vd, qg, kg, vg = _proj(
            stream, t, modsel, win_p, tabs, mla_q_norm[l][None, :], mla_kv_norm[l][None, :], wq, wkv, gq, gk, nct)

        ym = _attention(qm, km, vm, (), _mla_heads, joff, nct, "attn_mla")
        yd = _attention(qd, kd, vd, (diff_lambda[l], jnp.tile(diff_subln[l], 2)[None, :]),
                        functools.partial(_diff_heads, lam_init), joff, nct, "attn_diff")
        yg = _attention(qg, kg, vg, (), _gqa_heads, joff, nct, "attn_gqa")

        wr = jnp.zeros((d, LANES), F32).at[:, :N_EXPERTS].set(moe_w_sub[l])
        wr = wr.at[:, N_EXPERTS:N_EXPERTS + N_GROUPS].set(moe_w_group[l])
        wr_hi = wr.astype(BF16)
        wr = jnp.concatenate([wr_hi, (wr - wr_hi.astype(F32)).astype(BF16)], axis=1)
        br = jnp.zeros((1, LANES), F32).at[0, :N_EXPERTS].set(moe_b_sub[l])
        br = br.at[0, N_EXPERTS:N_EXPERTS + N_GROUPS].set(moe_b_group[l])
        x1, tok, route, counts = _post(stream, t, modsel, gb, zc, ym, yd, yg, conv_w[l], w_out[l].astype(BF16),
                                       ln1_g[l][None, :], ln1_b[l][None, :], wr, br, joff, nct, alpha)

        n_tok = x1.shape[0] * x1.shape[1]
        block_e, offs = _slot_plan(route.reshape(n_tok, LANES), counts[0, :N_EXPERTS], n_tok)
        ys = _experts(block_e, offs, tok.reshape(n_tok * SUBLANES, LANES), moe_w1, moe_w3, moe_w2, l)
        xnext = _combine(x1, modsel, route, ys, ln2_g[l][None, :], ln2_b[l][None, :], joff, nct, alpha)
        stream = (xnext, xnext, nct)
    return xnext
```

```python
import functools
import math

import jax
import jax.numpy as jnp
import numpy as np
from jax import lax
from jax.experimental import pallas as pl
from jax.experimental.pallas import tpu as pltpu

F32 = jnp.float32
BF16 = jnp.bfloat16

GRID_W = 64
ROPE_THETA = 10000.0
NORM_EPS = 1e-6
CONV_CH = 256
MLA_HEADS, MLA_Q_LORA, MLA_KV_LORA, MLA_NOPE, MLA_ROPE, MLA_V = 4, 256, 128, 64, 32, 64
DIFF_HEADS, DIFF_QK, DIFF_V = 4, 32, 64
GQA_Q_HEADS, GQA_KV_HEADS, GQA_HD = 4, 2, 64
IN_CONV = 3 * CONV_CH
IN_MLA = MLA_Q_LORA + MLA_KV_LORA + MLA_ROPE
DIFF_QW = DIFF_HEADS * 2 * DIFF_QK
IN_DIFF = 2 * DIFF_QW + DIFF_HEADS * DIFF_V
GQA_QW = GQA_Q_HEADS * GQA_HD
GQA_KW = GQA_KV_HEADS * GQA_HD
OFF_MLA = IN_CONV
OFF_DIFF = OFF_MLA + IN_MLA
OFF_GQA = OFF_DIFF + IN_DIFF
MLA_SCALE = (MLA_NOPE + MLA_ROPE) ** -0.5
DIFF_SCALE = DIFF_QK ** -0.5
GQA_SCALE = GQA_HD ** -0.5
N_GROUPS, EXPERTS_PER_GROUP = 4, 8
N_EXPERTS = N_GROUPS * EXPERTS_PER_GROUP
D_EXPERT = 256
LOG2E = math.log2(math.e)

LANES = 128
SUBLANES = 8
TM = 256
EB = 256
VMEM_LIMIT = 56 * 1024 * 1024

P_CONV = 0
P_MLA = P_CONV + 768
P_DIFF = P_MLA + 512
P_GQA = P_DIFF + 1024
P_END = P_GQA + 896


def _cparams(sem, flags=None):
    return pltpu.CompilerParams(dimension_semantics=sem, vmem_limit_bytes=VMEM_LIMIT, flags=flags)


def _lane(shape):
    return lax.broadcasted_iota(jnp.int32, shape, len(shape) - 1)


def _mod_kernel(c_ref, w_ref, b_ref, o_ref):
    c = c_ref[...]
    sc = (c * jax.nn.sigmoid(c)).astype(BF16)
    o_ref[0] = jnp.dot(sc, w_ref[0].astype(BF16), preferred_element_type=F32) + b_ref[0]


def _modulation(cc, w_ada, b_ada):
    n_layers, d, d6 = w_ada.shape
    r = cc.shape[0]
    return pl.pallas_call(
        _mod_kernel,
        grid=(n_layers, d6 // d),
        in_specs=[pl.BlockSpec((r, d), lambda l, n: (0, 0)),
                  pl.BlockSpec((1, d, d), lambda l, n: (l, 0, n)),
                  pl.BlockSpec((1, 1, d), lambda l, n: (l, 0, n))],
        out_specs=pl.BlockSpec((1, r, d), lambda l, n: (l, 0, n)),
        out_shape=jax.ShapeDtypeStruct((n_layers, r, d6), F32),
        compiler_params=_cparams(("arbitrary", "arbitrary")),
        name="mod",
    )(cc, w_ada, b_ada.reshape(n_layers, 1, d6))


def _rope(z, cos, sin_signed, half):
    n = z.shape[-1]
    first = (_lane(z.shape) % (2 * half)) < half
    partner = jnp.where(first, pltpu.roll(z, n - half, 1), pltpu.roll(z, half, 1))
    return z * cos + partner * sin_signed


def _rms(z, gain):
    return z * lax.rsqrt(jnp.mean(z * z, axis=-1, keepdims=True) + NORM_EPS) * gain


def _seg_rms64(z, gain):
    lo = _lane(z.shape) < 64
    z2 = z * z
    s_lo = jnp.sum(jnp.where(lo, z2, 0.0), axis=-1, keepdims=True)
    s_hi = jnp.sum(jnp.where(lo, 0.0, z2), axis=-1, keepdims=True)
    ms = jnp.where(lo, s_lo, s_hi) * (1.0 / 64)
    return z * lax.rsqrt(ms + NORM_EPS) * gain


def _ones_lane(odd):
    return (_lane((1, LANES)) == (0 if odd else 64)).astype(F32)


def _samples_per_step(batch):
    return 2 if batch % 2 == 0 else 1


def _stream_tile(xa_ref, xb_ref, jj, nct):
    return jnp.where(jj < nct, xa_ref[...], xb_ref[...])


def _stream_specs(stream, tb, joff, nct):
    xa, xb, lat_off = stream
    d = xa.shape[2]
    return [pl.BlockSpec((tb, TM, d), lambda bi, j: (bi, jnp.minimum(j + joff, nct - 1), 0)),
            pl.BlockSpec((tb, TM, d), lambda bi, j: (bi, jnp.maximum(j + joff - nct, 0) + lat_off, 0))]


def _put_rows(ref, lanes, val):
    for u in range(ref.shape[0]):
        ref[u, :, lanes] = val[u * TM:(u + 1) * TM]


def _proj_kernel(xa_ref, xb_ref, mod_ref, win_ref, cm_ref, sm_ref, cd_ref, sd_ref, cg_ref, sg_ref,
                 qn_ref, kvn_ref, wq_ref, wkv_ref, gq_ref, gk_ref,
                 gb_ref, zc_ref, qm_ref, km_ref, vm_ref, qd_ref, kd_ref, vd_ref, qg_ref, kg_ref, vg_ref, *, nct):
    x = _stream_tile(xa_ref, xb_ref, pl.program_id(1), nct)
    tb, _, d = x.shape
    sh1 = mod_ref[:, 0, 0:1, :]
    sc1 = mod_ref[:, 0, 1:2, :]
    xm = (x * (1.0 + sc1) + sh1).reshape(tb * TM, d).astype(BF16)
    everything = slice(None)

    def table(ref):
        return jnp.concatenate([ref[...]] * tb, axis=0)

    uc = jnp.dot(xm, win_ref[:, P_CONV:P_MLA], preferred_element_type=F32)
    _put_rows(gb_ref, everything, uc[:, 0:256])
    _put_rows(zc_ref, everything, uc[:, 256:512] * uc[:, 512:768])

    um = jnp.dot(xm, win_ref[:, P_MLA:P_DIFF], preferred_element_type=F32)
    cm, sm = table(cm_ref), table(sm_ref)
    cq = _rms(um[:, 0:256], qn_ref[...]).astype(BF16)
    qall = jnp.dot(cq, wq_ref[...], preferred_element_type=F32)
    ckv = _rms(um[:, 256:384], kvn_ref[...]).astype(BF16)
    kv = jnp.dot(ckv, wkv_ref[...], preferred_element_type=F32)
    kr = _rope(um[:, 384:512], cm, sm, 8)
    for h in range(MLA_HEADS):
        sl = slice(LANES * h, LANES * (h + 1))
        _put_rows(qm_ref, sl, (_rope(qall[:, sl], cm, sm, 8) * (MLA_SCALE * LOG2E)).astype(BF16))
        _put_rows(km_ref, sl, (kv[:, sl] + kr).astype(BF16))
        _put_rows(vm_ref, sl, (kv[:, 512 + LANES * h:512 + LANES * (h + 1)] + _ones_lane(h % 2)).astype(BF16))

    ud = jnp.dot(xm, win_ref[:, P_DIFF:P_GQA], preferred_element_type=F32)
    cd, sd = table(cd_ref), table(sd_ref)
    for c in range(2):
        sl = slice(LANES * c, LANES * (c + 1))
        _put_rows(qd_ref, sl, (_rope(ud[:, sl], cd, sd, 8) * (DIFF_SCALE * LOG2E)).astype(BF16))
        _put_rows(kd_ref, sl, _rope(ud[:, 256 + LANES * c:256 + LANES * (c + 1)], cd, sd, 8).astype(BF16))
    for h in range(DIFF_HEADS):
        sl = slice(LANES * h, LANES * (h + 1))
        _put_rows(vd_ref, sl, (ud[:, 512 + LANES * h:512 + LANES * (h + 1)] + _ones_lane(h % 2)).astype(BF16))

    ug = jnp.dot(xm, win_ref[:, P_GQA:P_END], preferred_element_type=F32)
    cg, sg = table(cg_ref), table(sg_ref)
    for c in range(2):
        sl = slice(LANES * c, LANES * (c + 1))
        _put_rows(qg_ref, sl,
                  (_rope(_seg_rms64(ug[:, sl], gq_ref[...]), cg, sg, 16) * (GQA_SCALE * LOG2E)).astype(BF16))
    _put_rows(kg_ref, everything, _rope(_seg_rms64(ug[:, 256:384], gk_ref[...]), cg, sg, 16).astype(BF16))
    for i in range(4):
        sl = slice(LANES * i, LANES * (i + 1))
        _put_rows(vg_ref, sl, (ug[:, 384 + LANES * i:384 + LANES * (i + 1)] + _ones_lane(i % 2)).astype(BF16))


def _proj(stream, t, modsel, win_p, tabs, qn, kvn, wq, wkv, gq, gk, nct):
    b, _, d = stream[0].shape
    nt = t // TM
    tb = 1

    def tile(width):
        return pl.BlockSpec((tb, TM, width), lambda bi, j: (bi, j, 0))

    def full(arr):
        return pl.BlockSpec(arr.shape, lambda bi, j: (0,) * arr.ndim)

    tab_spec = pl.BlockSpec((TM, LANES), lambda bi, j: (j, 0))
    widths = [(256, F32), (256, F32), (512, BF16), (512, BF16), (512, BF16), (256, BF16), (256, BF16),
              (512, BF16), (256, BF16), (128, BF16), (512, BF16)]
    return pl.pallas_call(
        functools.partial(_proj_kernel, nct=nct),
        grid=(b // tb, nt),
        in_specs=_stream_specs(stream, tb, 0, nct) + [
                  pl.BlockSpec((tb, 1, 6, d), lambda bi, j: (bi, jnp.minimum(j // nct, 1), 0, 0)),
                  full(win_p)] + [tab_spec] * 6 + [full(a) for a in (qn, kvn, wq, wkv, gq, gk)],
        out_specs=[tile(w) for w, _ in widths],
        out_shape=[jax.ShapeDtypeStruct((b, t, w), dt) for w, dt in widths],
        compiler_params=_cparams(("arbitrary", "arbitrary")),
        name="proj",
    )(stream[0], stream[1], modsel, win_p, *tabs, qn, kvn, wq, wkv, gq, gk)


def _scores(q_ref, k_ref, tk, chunk, mask):
    q = q_ref[0, :, LANES * chunk:LANES * (chunk + 1)]
    if mask is not None:
        lane = _lane(q.shape)
        q = jnp.where((lane >= mask[0]) & (lane < mask[1]), q, jnp.zeros_like(q))
    k = k_ref[0, 0:tk, LANES * chunk:LANES * (chunk + 1)]
    return lax.dot_general(q, k, (((1,), (1,)), ((), ())), preferred_element_type=F32)


def _attend(s, v, odd):
    p = jnp.exp2(s - jnp.max(s, axis=-1, keepdims=True)).astype(BF16)
    o = jnp.dot(p, v, preferred_element_type=F32)
    den = o[:, 0:1] if odd else o[:, 64:65]
    return o * (1.0 / den)


def _pair(even, odd):
    return jnp.where(_lane(even.shape) < 64, even, odd)


def _vblock(v_ref, tk, idx):
    return v_ref[0, 0:tk, LANES * idx:LANES * (idx + 1)]


def _one_ahead(n, score_fn, finish_fn):
    outs, nxt = [], score_fn(0)
    for h in range(n):
        cur = nxt
        if h + 1 < n:
            nxt = score_fn(h + 1)
        outs.append(finish_fn(h, cur))
    return outs


def _mla_heads(q_ref, k_ref, v_ref, tk, extra):
    o = _one_ahead(MLA_HEADS, lambda h: _scores(q_ref, k_ref, tk, h, None),
                   lambda h, s: _attend(s, _vblock(v_ref, tk, h), h % 2))
    return jnp.concatenate([_pair(o[0], o[1]), _pair(o[2], o[3])], axis=-1)


def _gqa_heads(q_ref, k_ref, v_ref, tk, extra):
    def scores(i):
        kvh, g = divmod(i, GQA_Q_HEADS // GQA_KV_HEADS)
        q = q_ref[0, :, LANES * g:LANES * (g + 1)]
        lane = _lane(q.shape)
        q = jnp.where((lane >= 64 * kvh) & (lane < 64 * kvh + 64), q, jnp.zeros_like(q))
        return lax.dot_general(q, k_ref[0, 0:tk, :], (((1,), (1,)), ((), ())), preferred_element_type=F32)

    o = _one_ahead(GQA_Q_HEADS, scores, lambda i, s: _attend(s, _vblock(v_ref, tk, i), i % 2))
    return jnp.concatenate([_pair(o[0], o[1]), _pair(o[2], o[3])], axis=-1)


def _diff_heads(lam_init, q_ref, k_ref, v_ref, tk, extra):
    dl_ref, subln_ref = extra
    dl = dl_ref[...]
    lam = (jnp.exp(jnp.sum(dl[0:1] * dl[1:2], axis=-1, keepdims=True))
           - jnp.exp(jnp.sum(dl[2:3] * dl[3:4], axis=-1, keepdims=True)) + lam_init)

    def head_scores(h):
        return [_scores(q_ref, k_ref, tk, (2 * h + m) // 4, (32 * ((2 * h + m) % 4), 32 * ((2 * h + m) % 4) + 32))
                for m in range(2)]

    def finish(h, pair):
        probs = []
        for s in pair:
            e = jnp.exp2(s - jnp.max(s, axis=-1, keepdims=True))
            probs.append((e, jnp.sum(e, axis=-1, keepdims=True)))
        inv_l0 = 1.0 / probs[0][1]
        a = probs[0][0] - probs[1][0] * (lam * probs[0][1] / probs[1][1])
        return jnp.dot(a.astype(BF16), _vblock(v_ref, tk, h), preferred_element_type=F32) * inv_l0

    d = _one_ahead(DIFF_HEADS, head_scores, finish)
    chunks = [_seg_rms64(_pair(d[2 * c], d[2 * c + 1]), subln_ref[...]) * (1.0 - lam_init) for c in range(2)]
    return jnp.concatenate(chunks, axis=-1)


def _attn_kernel(*refs, heads, joff, nct, ctx_len, total_len):
    q_ref, k_ref, v_ref = refs[:3]
    extra, o_ref = refs[3:-1], refs[-1]
    jj = pl.program_id(1) + joff

    def run(tk):
        o_ref[0] = heads(q_ref, k_ref, v_ref, tk, extra).astype(o_ref.dtype)

    if joff < nct:
        pl.when(jj < nct)(lambda: run(ctx_len))
        pl.when(jj >= nct)(lambda: run(total_len))
    else:
        run(total_len)


def _attention(q, k, v, extra, heads, joff, nct, name):
    b, t, _ = q.shape
    nt = t // TM

    def full(arr):
        return pl.BlockSpec(arr.shape, lambda bi, j: (0,) * arr.ndim)

    kern = functools.partial(_attn_kernel, heads=heads, joff=joff, nct=nct, ctx_len=nct * TM, total_len=t)
    return pl.pallas_call(
        kern,
        grid=(b, nt - joff),
        in_specs=[pl.BlockSpec((1, TM, q.shape[2]), lambda bi, j: (bi, j + joff, 0)),
                  pl.BlockSpec((1, t, k.shape[2]), lambda bi, j: (bi, 0, 0)),
                  pl.BlockSpec((1, t, v.shape[2]), lambda bi, j: (bi, 0, 0))] + [full(a) for a in extra],
        out_specs=pl.BlockSpec((1, TM, 256), lambda bi, j: (bi, j, 0)),
        out_shape=jax.ShapeDtypeStruct((b, t - joff * TM, 256), BF16),
        compiler_params=_cparams(("arbitrary", "arbitrary")),
        name=name,
    )(q, k, v, *extra)


def _layer_norm(h, g, bias):
    mu = jnp.mean(h, axis=-1, keepdims=True)
    hc = h - mu
    var = jnp.mean(hc * hc, axis=-1, keepdims=True)
    return hc * lax.rsqrt(var + NORM_EPS) * g + bias


def _store_token_tiles(ref, val):
    rows = val.shape[0]
    for c in range(SUBLANES):
        ref[pl.ds(c, rows, stride=SUBLANES), :] = val[:, LANES * c:LANES * (c + 1)]


def _load_token_tiles(ref, rows):
    return jnp.concatenate([ref[pl.ds(c, rows, stride=SUBLANES), :] for c in range(SUBLANES)], axis=1)


def _min_lane(cond, lane_f):
    return jnp.min(jnp.where(cond, lane_f, float(LANES)), axis=-1, keepdims=True)


def _post_kernel(xa_ref, xb_ref, mod_ref, gb_ref, zc_ref, zp_ref, zn_ref, ym_ref, yd_ref, yg_ref, cw_ref, wout_ref,
                 g_ref, b_ref, wr_ref, br_ref, x1_ref, tok_ref, route_ref, cnt_ref, cnt_acc,
                 *, joff, nct, nt, alpha):
    jj = pl.program_id(1) + joff
    first_step = (pl.program_id(0) == 0) & (pl.program_id(1) == 0)

    @pl.when(first_step)
    def _():
        cnt_acc[...] = jnp.zeros_like(cnt_acc)

    tb = zc_ref.shape[0]
    rows = tb * TM
    row = lax.broadcasted_iota(jnp.int32, (TM, zc_ref.shape[2]), 0)
    left_ok = (jj != 0) & (jj != nct)
    right_ok = (jj != nct - 1) & (jj != nt - 1)
    cw = cw_ref[...]
    mixed = []
    for u in range(tb):
        zc = zc_ref[u]
        halo_prev = jnp.where(left_ok, zp_ref[u, SUBLANES - 1:SUBLANES, :], 0.0)
        halo_next = jnp.where(right_ok, zn_ref[u, 0:1, :], 0.0)
        zprev = jnp.where(row == 0, halo_prev, pltpu.roll(zc, 1, 0))
        znext = jnp.where(row == TM - 1, halo_next, pltpu.roll(zc, TM - 1, 0))
        conv = zprev * cw[0:1] + zc * cw[1:2] + znext * cw[2:3]
        mixed.append(jnp.concatenate([(gb_ref[u] * conv).astype(BF16), ym_ref[u], yd_ref[u], yg_ref[u]], axis=-1))
    y = jnp.concatenate(mixed, axis=0)
    acc = jnp.dot(y, wout_ref[...], preferred_element_type=F32)
    d = acc.shape[1]

    g1 = mod_ref[:, 0, 2:3, :]
    sh2 = mod_ref[:, 0, 3:4, :]
    sc2 = mod_ref[:, 0, 4:5, :]
    x1 = _layer_norm(alpha * _stream_tile(xa_ref, xb_ref, jj, nct) + g1 * acc.reshape(tb, TM, d),
                     g_ref[...], b_ref[...])
    x1_ref[...] = x1
    tok3 = x1 * (1.0 + sc2) + sh2
    for u in range(tb):
        _store_token_tiles(tok_ref.at[u], tok3[u])
    tok = tok3.reshape(rows, d)

    tok_hi = tok.astype(BF16)
    tok_lo = (tok - tok_hi.astype(F32)).astype(BF16)
    part = jnp.dot(tok_hi, wr_ref[...], preferred_element_type=F32)
    logits = (part[:, 0:LANES] + (part[:, LANES:2 * LANES]
                                  + jnp.dot(tok_lo, wr_ref[:, 0:LANES], preferred_element_type=F32))) + br_ref[...]
    lane = _lane(logits.shape)
    lane_f = lane.astype(F32)
    neg = -jnp.inf
    is_g = (lane >= N_EXPERTS) & (lane < N_EXPERTS + N_GROUPS)
    lg = jnp.where(is_g, logits, neg)
    eg = jnp.exp(lg - jnp.max(lg, axis=-1, keepdims=True))
    pg = eg / jnp.sum(eg, axis=-1, keepdims=True)
    g_w = jnp.max(pg, axis=-1, keepdims=True)
    g_lane = _min_lane(is_g & (pg == g_w), lane_f)
    g_idx = g_lane - float(N_EXPERTS)
    in_grp = (lane < N_EXPERTS) & ((lane // EXPERTS_PER_GROUP).astype(F32) == g_idx)
    ls = jnp.where(in_grp, logits, neg)
    es = jnp.exp(ls - jnp.max(ls, axis=-1, keepdims=True))
    ps = es / jnp.sum(es, axis=-1, keepdims=True)
    p1 = jnp.max(jnp.where(in_grp, ps, -1.0), axis=-1, keepdims=True)
    i1 = _min_lane(in_grp & (ps == p1), lane_f)
    rest = in_grp & (lane_f != i1)
    p2 = jnp.max(jnp.where(rest, ps, -1.0), axis=-1, keepdims=True)
    i2 = _min_lane(rest & (ps == p2), lane_f)
    tot = p1 + p2
    w1 = g_w * (p1 / tot)
    w2 = g_w * (p2 / tot)

    oh1 = lane_f == i1
    oh2 = lane_f == i2
    r_i = lax.broadcasted_iota(jnp.int32, (rows, rows), 0)
    c_i = lax.broadcasted_iota(jnp.int32, (rows, rows), 1)
    tri = (r_i > c_i).astype(BF16)
    before1 = jnp.dot(tri, oh1.astype(BF16), preferred_element_type=F32)
    before2 = jnp.dot(tri, oh2.astype(BF16), preferred_element_type=F32)
    tot1 = jnp.sum(oh1.astype(F32), axis=0, keepdims=True)
    tot2 = jnp.sum(oh2.astype(F32), axis=0, keepdims=True)
    base = cnt_acc[...]
    r1 = jnp.sum(jnp.where(oh1, base + before1, 0.0), axis=-1, keepdims=True)
    r2 = jnp.sum(jnp.where(oh2, base + tot1 + before2, 0.0), axis=-1, keepdims=True)
    new_cnt = base + tot1 + tot2
    cnt_acc[...] = new_cnt
    cnt_ref[...] = jnp.broadcast_to(new_cnt, cnt_ref.shape)

    vals = (i1, i2, w1, w2, r1, r2)
    route = jnp.zeros(logits.shape, F32)
    for idx, val in enumerate(vals):
        route = jnp.where(lane == idx, val, route)
    _put_rows(route_ref, slice(None), route)


def _post(stream, t, modsel, gb, zc, ym, yd, yg, conv_w, wout, ln_g, ln_b, wr, br, joff, nct, alpha):
    b, _, d = stream[0].shape
    nt = t // TM
    nj = nt - joff
    hb = TM // SUBLANES
    tb = _samples_per_step(b)

    def tile(width):
        return pl.BlockSpec((tb, TM, width), lambda bi, j: (bi, j + joff, 0))

    def otile(width):
        return pl.BlockSpec((tb, TM, width), lambda bi, j: (bi, j, 0))

    def full(arr):
        return pl.BlockSpec(arr.shape, lambda bi, j: (0,) * arr.ndim)

    kern = functools.partial(_post_kernel, joff=joff, nct=nct, nt=nt, alpha=alpha)
    return pl.pallas_call(
        kern,
        grid=(b // tb, nj),
        in_specs=_stream_specs(stream, tb, joff, nct) + [
                  pl.BlockSpec((tb, 1, 6, d), lambda bi, j: (bi, jnp.minimum((j + joff) // nct, 1), 0, 0)),
                  tile(256), tile(256),
                  pl.BlockSpec((tb, SUBLANES, 256), lambda bi, j: (bi, jnp.maximum((j + joff) * hb - 1, 0), 0)),
                  pl.BlockSpec((tb, SUBLANES, 256),
                               lambda bi, j: (bi, jnp.minimum((j + joff + 1) * hb, nt * hb - 1), 0)),
                  otile(256), otile(256), otile(256),
                  full(conv_w), full(wout), full(ln_g), full(ln_b), full(wr), full(br)],
        out_specs=[otile(d), pl.BlockSpec((tb, TM * SUBLANES, LANES), lambda bi, j: (bi, j, 0)),
                   otile(LANES), pl.BlockSpec((SUBLANES, LANES), lambda bi, j: (0, 0))],
        out_shape=[jax.ShapeDtypeStruct((b, nj * TM, d), F32),
                   jax.ShapeDtypeStruct((b, nj * TM * SUBLANES, LANES), F32),
                   jax.ShapeDtypeStruct((b, nj * TM, LANES), F32),
                   jax.ShapeDtypeStruct((SUBLANES, LANES), F32)],
        scratch_shapes=[pltpu.VMEM((1, LANES), F32)],
        compiler_params=_cparams(("arbitrary", "arbitrary")),
        name="post",
    )(stream[0], stream[1], modsel, gb, zc, zc, zc, ym, yd, yg, conv_w, wout, ln_g, ln_b, wr, br)


ROW_UNROLL = EB
TILE_ROWS = SUBLANES


def _expert_kernel(be_ref, offs_ref, tok_ref, w1_ref, w3_ref, w2_ref, out_ref, xbuf, ybuf, gsem, ssem,
                   *, n_tok, nblk):
    del be_ref
    i = pl.program_id(0)
    slot = i % 2
    blk_rows = EB * TILE_ROWS

    def tile_at(ref, first_row):
        return ref.at[pl.ds(pl.multiple_of(first_row, TILE_ROWS), TILE_ROWS)]

    def gather_start(blk, sl):
        def body(g, carry):
            for u in range(ROW_UNROLL):
                r = g * ROW_UNROLL + u
                pltpu.make_async_copy(tile_at(tok_ref, offs_ref[blk, r]), tile_at(xbuf.at[sl], r * TILE_ROWS),
                                      gsem.at[sl]).start(priority=u % 2)
            return carry
        lax.fori_loop(0, EB // ROW_UNROLL, body, 0)

    def scatter_start(blk, sl):
        def body(g, carry):
            for u in range(ROW_UNROLL):
                r = g * ROW_UNROLL + u
                pltpu.make_async_copy(tile_at(ybuf.at[sl], r * TILE_ROWS), tile_at(out_ref, offs_ref[blk, EB + r]),
                                      ssem.at[sl]).start(priority=u % 2)
            return carry
        lax.fori_loop(0, EB // ROW_UNROLL, body, 0)

    def gather_wait(sl):
        pltpu.make_async_copy(tok_ref.at[pl.ds(0, blk_rows)], xbuf.at[sl], gsem.at[sl]).wait()

    def scatter_wait(sl):
        pltpu.make_async_copy(ybuf.at[sl], out_ref.at[pl.ds(0, blk_rows)], ssem.at[sl]).wait()

    @pl.when(i == 0)
    def _():
        gather_start(0, 0)

    @pl.when(i + 1 < nblk)
    def _():
        gather_start(i + 1, 1 - slot)

    gather_wait(slot)

    @pl.when(i >= 2)
    def _():
        scatter_wait(slot)

    xb = _load_token_tiles(xbuf.at[slot], EB).astype(BF16)
    h1 = jnp.dot(xb, w1_ref[0, 0].astype(BF16), preferred_element_type=F32)
    h3 = jnp.dot(xb, w3_ref[0, 0].astype(BF16), preferred_element_type=F32)
    a = (h1 * jax.nn.sigmoid(h1) * h3).astype(BF16)
    _store_token_tiles(ybuf.at[slot], jnp.dot(a, w2_ref[0, 0].astype(BF16), preferred_element_type=F32))
    scatter_start(i, slot)

    @pl.when(i == nblk - 1)
    def _():
        scatter_wait(1 - slot)
        scatter_wait(slot)
        ybuf[0] = jnp.zeros(ybuf.shape[1:], F32)
        for half in range(2):
            cp = pltpu.make_async_copy(
                ybuf.at[0], out_ref.at[pl.ds((2 * n_tok + half * EB) * TILE_ROWS, blk_rows)], ssem.at[0])
            cp.start()
            cp.wait()


def _experts(block_e, offs, tok_tiles, w1, w3, w2, layer):
    n_tok = tok_tiles.shape[0] // TILE_ROWS
    d, de = w1.shape[2], w1.shape[3]
    nblk = offs.shape[0]
    assert nblk >= 2 and d == TILE_ROWS * LANES
    kern = functools.partial(_expert_kernel, n_tok=n_tok, nblk=nblk)
    return pl.pallas_call(
        kern,
        grid_spec=pltpu.PrefetchScalarGridSpec(
            num_scalar_prefetch=2,
            grid=(nblk,),
            in_specs=[pl.BlockSpec(memory_space=pl.ANY),
                      pl.BlockSpec((1, 1, d, de), lambda i, be, offs: (layer, be[i], 0, 0)),
                      pl.BlockSpec((1, 1, d, de), lambda i, be, offs: (layer, be[i], 0, 0)),
                      pl.BlockSpec((1, 1, de, d), lambda i, be, offs: (layer, be[i], 0, 0))],
            out_specs=pl.BlockSpec(memory_space=pl.ANY),
            scratch_shapes=[pltpu.VMEM((2, EB * TILE_ROWS, LANES), F32), pltpu.VMEM((2, EB * TILE_ROWS, LANES), F32),
                            pltpu.SemaphoreType.DMA((2,)), pltpu.SemaphoreType.DMA((2,))]),
        out_shape=jax.ShapeDtypeStruct(((2 * n_tok + 2 * EB) * TILE_ROWS, LANES), F32),
        compiler_params=_cparams(("arbitrary",)),
        name="experts",
    )(block_e, offs, tok_tiles, w1, w3, w2)


def _combine_kernel(x1_ref, mod_ref, route_ref, f0_ref, f1_ref, g_ref, b_ref, o_ref, *, alpha):
    route = route_ref[0]
    f = route[:, 2:3] * _load_token_tiles(f0_ref, TM) + route[:, 3:4] * _load_token_tiles(f1_ref, TM)
    g2 = mod_ref[0, 0, 5:6, :]
    o_ref[0] = _layer_norm(alpha * x1_ref[0] + g2 * f, g_ref[...], b_ref[...])


def _combine(x1, modsel, route, ys, ln_g, ln_b, joff, nct, alpha):
    b, n, d = x1.shape
    nj = n // TM
    plane = b * nj

    def full(arr):
        return pl.BlockSpec(arr.shape, lambda bi, j: (0,) * arr.ndim)

    return pl.pallas_call(
        functools.partial(_combine_kernel, alpha=alpha),
        grid=(b, nj),
        in_specs=[pl.BlockSpec((1, TM, d), lambda bi, j: (bi, j, 0)),
                  pl.BlockSpec((1, 1, 6, d), lambda bi, j: (bi, jnp.minimum((j + joff) // nct, 1), 0, 0)),
                  pl.BlockSpec((1, TM, LANES), lambda bi, j: (bi, j, 0)),
                  pl.BlockSpec((TM * TILE_ROWS, LANES), lambda bi, j: (bi * nj + j, 0)),
                  pl.BlockSpec((TM * TILE_ROWS, LANES), lambda bi, j: (plane + bi * nj + j, 0)),
                  full(ln_g), full(ln_b)],
        out_specs=pl.BlockSpec((1, TM, d), lambda bi, j: (bi, j, 0)),
        out_shape=jax.ShapeDtypeStruct((b, n, d), F32),
        compiler_params=_cparams(("arbitrary", "arbitrary")),
        name="combine",
    )(x1, modsel, route, ys, ys, ln_g, ln_b)


def _rope_tables(seq, ctx_len):
    t = jnp.arange(seq, dtype=jnp.int32)
    row = (t // GRID_W).astype(F32)
    col = (t % GRID_W).astype(F32)

    def table(vec_dim):
        half = vec_dim // 4
        freqs = ROPE_THETA ** (-jnp.arange(half, dtype=F32) / half)
        lane = np.arange(vec_dim)
        idx = lane % half
        use_col = (lane // (2 * half)) == 1
        ang = jnp.where(use_col[None, :], col[:, None], row[:, None]) * freqs[idx][None, :]
        sign = np.where((lane % (2 * half)) < half, -1.0, 1.0).astype(np.float32)
        return jnp.cos(ang), jnp.sin(ang) * sign[None, :]

    def with_ctx(cos, sin):
        width = cos.shape[1]
        return (jnp.concatenate([jnp.ones((ctx_len, width), F32), cos], axis=0),
                jnp.concatenate([jnp.zeros((ctx_len, width), F32), sin], axis=0))

    c32, s32 = table(32)
    c64, s64 = table(64)
    cd, sd = with_ctx(jnp.tile(c32, (1, 4)), jnp.tile(s32, (1, 4)))
    cg, sg = with_ctx(jnp.tile(c64, (1, 2)), jnp.tile(s64, (1, 2)))
    ones, zeros = jnp.ones((seq, 64), F32), jnp.zeros((seq, 64), F32)
    cm, sm = with_ctx(jnp.concatenate([ones, c32, ones[:, :32]], axis=1),
                      jnp.concatenate([zeros, s32, zeros[:, :32]], axis=1))
    return cm, sm, cd, sd, cg, sg


def _in_proj_columns():
    def spread(start, odd_blocks):
        cols = []
        for i, odd in enumerate(odd_blocks):
            vals = list(range(start + 64 * i, start + 64 * (i + 1)))
            cols += ([-1] * 64 + vals) if odd else (vals + [-1] * 64)
        return cols

    cols = list(range(0, IN_CONV))
    cols += list(range(OFF_MLA, OFF_MLA + MLA_Q_LORA + MLA_KV_LORA))
    cols += [-1] * 64 + list(range(OFF_MLA + MLA_Q_LORA + MLA_KV_LORA, OFF_DIFF)) + [-1] * 32
    cols += list(range(OFF_DIFF, OFF_DIFF + 2 * DIFF_QW))
    cols += spread(OFF_DIFF + 2 * DIFF_QW, [h % 2 for h in range(DIFF_HEADS)])
    q0 = OFF_GQA
    head = lambda h: list(range(q0 + GQA_HD * h, q0 + GQA_HD * (h + 1)))
    cols += head(0) + head(2) + head(1) + head(3)
    cols += list(range(OFF_GQA + GQA_QW, OFF_GQA + GQA_QW + GQA_KW))
    v0 = OFF_GQA + GQA_QW + GQA_KW
    for kvh in range(GQA_KV_HEADS):
        vals = list(range(v0 + GQA_HD * kvh, v0 + GQA_HD * (kvh + 1)))
        cols += vals + [-1] * 64 + [-1] * 64 + vals
    assert len(cols) == P_END
    return np.asarray(cols, np.int32)


def _relayout_columns(w, cols):
    valid = jnp.asarray(cols >= 0)
    return jnp.where(valid[None, :], jnp.take(w, jnp.asarray(np.maximum(cols, 0)), axis=1), 0.0)


def _mla_weights(w_uq, w_qr, w_uk, w_uv):
    zq = jnp.zeros((MLA_Q_LORA, 32), F32)
    zk = jnp.zeros((MLA_KV_LORA, 64), F32)
    wq = jnp.concatenate([blk for h in range(MLA_HEADS)
                          for blk in (w_uq[:, 64 * h:64 * (h + 1)], w_qr[:, 32 * h:32 * (h + 1)], zq)], axis=1)
    wk = jnp.concatenate([blk for h in range(MLA_HEADS) for blk in (w_uk[:, 64 * h:64 * (h + 1)], zk)], axis=1)
    wv = jnp.concatenate([blk for h in range(MLA_HEADS)
                          for blk in ((zk, w_uv[:, 64 * h:64 * (h + 1)]) if h % 2 else
                                      (w_uv[:, 64 * h:64 * (h + 1)], zk))], axis=1)
    return wq.astype(BF16), jnp.concatenate([wk, wv], axis=1).astype(BF16)


def _slot_plan(route, counts, n_tok):
    n_slots = 2 * n_tok + N_EXPERTS * EB
    nblk = n_slots // EB
    cnt = counts.astype(jnp.int32)
    padded = (cnt + EB - 1) // EB * EB
    pad_ends = jnp.cumsum(padded)
    pad_starts = pad_ends - padded
    cols = route[:, 0:SUBLANES].T.astype(jnp.int32)
    dest = jnp.concatenate([pad_starts[cols[0]] + cols[4], pad_starts[cols[1]] + cols[5]])
    slot = jnp.arange(n_slots, dtype=jnp.int32)
    discard = 2 * n_tok + ((slot // EB) % 2) * EB + slot % EB
    res_row = discard.at[dest].set(jnp.arange(2 * n_tok, dtype=jnp.int32))
    src_row = res_row - n_tok * ((res_row >= n_tok).astype(jnp.int32) + (res_row >= 2 * n_tok).astype(jnp.int32))
    offs = jnp.concatenate([src_row.reshape(nblk, EB), res_row.reshape(nblk, EB)], axis=1) * TILE_ROWS
    block_start = jnp.arange(nblk, dtype=jnp.int32) * EB
    block_e = jnp.minimum(jnp.sum((pad_ends[None, :] <= block_start[:, None]).astype(jnp.int32), axis=1),
                          N_EXPERTS - 1)
    return block_e, offs


def kernel(x, c, ctx, c_ctx, w_ada, b_ada, w_in, w_out, conv_w, mla_q_norm, mla_kv_norm, mla_w_uq, mla_w_qr,
           mla_w_uk, mla_w_uv, diff_lambda, diff_subln, gqa_q_norm, gqa_k_norm, ln1_g, ln1_b, ln2_g, ln2_b,
           moe_w_group, moe_b_group, moe_w_sub, moe_b_sub, moe_w1, moe_w3, moe_w2):
    b, s, d = x.shape
    n_ctx = ctx.shape[1]
    depth = w_in.shape[0]
    assert n_ctx % TM == 0 and s % TM == 0 and s % GRID_W == 0 and d == SUBLANES * LANES
    nct = n_ctx // TM
    t = n_ctx + s
    alpha = (2 * depth) ** 0.25

    rows = ((b + 1 + SUBLANES - 1) // SUBLANES) * SUBLANES
    cc = jnp.zeros((rows, d), F32).at[:b].set(c).at[b].set(c_ctx)
    mod = _modulation(cc, w_ada, b_ada)
    tabs = _rope_tables(s, n_ctx)
    cols = _in_proj_columns()
    stream = (ctx, x, 0)

    for l in range(depth):
        last = l == depth - 1
        joff = nct if last else 0
        lam_init = 0.8 - 0.6 * math.exp(-0.3 * l)
        ml = mod[l].reshape(rows, 6, d)
        modsel = jnp.stack([jnp.broadcast_to(ml[b], (b, 6, d)), ml[:b]], axis=1)
        win_p = _relayout_columns(w_in[l], cols).astype(BF16)
        wq, wkv = _mla_weights(mla_w_uq[l], mla_w_qr[l], mla_w_uk[l], mla_w_uv[l])
        gq = jnp.tile(gqa_q_norm[l], 2)[None, :]
        gk = jnp.tile(gqa_k_norm[l], 2)[None, :]
        gb, zc, qm, km, vm, qd, kd, vd, qg, kg, vg = _proj(
            stream, t, modsel, win_p, tabs, mla_q_norm[l][None, :], mla_kv_norm[l][None, :], wq, wkv, gq, gk, nct)

        ym = _attention(qm, km, vm, (), _mla_heads, joff, nct, "attn_mla")
        yd = _attention(qd, kd, vd, (diff_lambda[l], jnp.tile(diff_subln[l], 2)[None, :]),
                        functools.partial(_diff_heads, lam_init), joff, nct, "attn_diff")
        yg = _attention(qg, kg, vg, (), _gqa_heads, joff, nct, "attn_gqa")

        wr = jnp.zeros((d, LANES), F32).at[:, :N_EXPERTS].set(moe_w_sub[l])
        wr = wr.at[:, N_EXPERTS:N_EXPERTS + N_GROUPS].set(moe_w_group[l])
        wr_hi = wr.astype(BF16)
        wr = jnp.concatenate([wr_hi, (wr - wr_hi.astype(F32)).astype(BF16)], axis=1)
        br = jnp.zeros((1, LANES), F32).at[0, :N_EXPERTS].set(moe_b_sub[l])
        br = br.at[0, N_EXPERTS:N_EXPERTS + N_GROUPS].set(moe_b_group[l])
        x1, tok, route, counts = _post(stream, t, modsel, gb, zc, ym, yd, yg, conv_w[l], w_out[l].astype(BF16),
                                       ln1_g[l][None, :], ln1_b[l][None, :], wr, br, joff, nct, alpha)

        n_tok = x1.shape[0] * x1.shape[1]
        block_e, offs = _slot_plan(route.reshape(n_tok, LANES), counts[0, :N_EXPERTS], n_tok)
        ys = _experts(block_e, offs, tok.reshape(n_tok * SUBLANES, LANES), moe_w1, moe_w3, moe_w2, l)
        xnext = _combine(x1, modsel, route, ys, ln2_g[l][None, :], ln2_b[l][None, :], joff, nct, alpha)
        stream = (xnext, xnext, nct)
    return xnext
```

```python
import functools
import math

import jax
import jax.numpy as jnp
import numpy as np
from jax import lax
from jax.experimental import pallas as pl
from jax.experimental.pallas import tpu as pltpu

F32 = jnp.float32
BF16 = jnp.bfloat16

GRID_W = 64
ROPE_THETA = 10000.0
NORM_EPS = 1e-6
CONV_CH = 256
MLA_HEADS, MLA_Q_LORA, MLA_KV_LORA, MLA_NOPE, MLA_ROPE, MLA_V = 4, 256, 128, 64, 32, 64
DIFF_HEADS, DIFF_QK, DIFF_V = 4, 32, 64
GQA_Q_HEADS, GQA_KV_HEADS, GQA_HD = 4, 2, 64
IN_CONV = 3 * CONV_CH
IN_MLA = MLA_Q_LORA + MLA_KV_LORA + MLA_ROPE
DIFF_QW = DIFF_HEADS * 2 * DIFF_QK
IN_DIFF = 2 * DIFF_QW + DIFF_HEADS * DIFF_V
GQA_QW = GQA_Q_HEADS * GQA_HD
GQA_KW = GQA_KV_HEADS * GQA_HD
OFF_MLA = IN_CONV
OFF_DIFF = OFF_MLA + IN_MLA
OFF_GQA = OFF_DIFF + IN_DIFF
MLA_SCALE = (MLA_NOPE + MLA_ROPE) ** -0.5
DIFF_SCALE = DIFF_QK ** -0.5
GQA_SCALE = GQA_HD ** -0.5
N_GROUPS, EXPERTS_PER_GROUP = 4, 8
N_EXPERTS = N_GROUPS * EXPERTS_PER_GROUP
D_EXPERT = 256
LOG2E = math.log2(math.e)

LANES = 128
SUBLANES = 8
TM = 256
EB = 256
VMEM_LIMIT = 56 * 1024 * 1024

P_CONV = 0
P_MLA = P_CONV + 768
P_DIFF = P_MLA + 512
P_GQA = P_DIFF + 1024
P_END = P_GQA + 896


def _cparams(sem, flags=None):
    return pltpu.CompilerParams(dimension_semantics=sem, vmem_limit_bytes=VMEM_LIMIT, flags=flags)


def _lane(shape):
    return lax.broadcasted_iota(jnp.int32, shape, len(shape) - 1)


def _mod_kernel(c_ref, w_ref, b_ref, o_ref):
    c = c_ref[...]
    sc = (c * jax.nn.sigmoid(c)).astype(BF16)
    o_ref[0] = jnp.dot(sc, w_ref[0].astype(BF16), preferred_element_type=F32) + b_ref[0]


def _modulation(cc, w_ada, b_ada):
    n_layers, d, d6 = w_ada.shape
    r = cc.shape[0]
    return pl.pallas_call(
        _mod_kernel,
        grid=(n_layers, d6 // d),
        in_specs=[pl.BlockSpec((r, d), lambda l, n: (0, 0)),
                  pl.BlockSpec((1, d, d), lambda l, n: (l, 0, n)),
                  pl.BlockSpec((1, 1, d), lambda l, n: (l, 0, n))],
        out_specs=pl.BlockSpec((1, r, d), lambda l, n: (l, 0, n)),
        out_shape=jax.ShapeDtypeStruct((n_layers, r, d6), F32),
        compiler_params=_cparams(("arbitrary", "arbitrary")),
        name="mod",
    )(cc, w_ada, b_ada.reshape(n_layers, 1, d6))


def _rope(z, cos, sin_signed, half):
    n = z.shape[-1]
    first = (_lane(z.shape) % (2 * half)) < half
    partner = jnp.where(first, pltpu.roll(z, n - half, 1), pltpu.roll(z, half, 1))
    return z * cos + partner * sin_signed


def _rms(z, gain):
    return z * lax.rsqrt(jnp.mean(z * z, axis=-1, keepdims=True) + NORM_EPS) * gain


def _seg_rms64(z, gain):
    lo = _lane(z.shape) < 64
    z2 = z * z
    s_lo = jnp.sum(jnp.where(lo, z2, 0.0), axis=-1, keepdims=True)
    s_hi = jnp.sum(jnp.where(lo, 0.0, z2), axis=-1, keepdims=True)
    ms = jnp.where(lo, s_lo, s_hi) * (1.0 / 64)
    return z * lax.rsqrt(ms + NORM_EPS) * gain


def _ones_lane(odd):
    return (_lane((1, LANES)) == (0 if odd else 64)).astype(F32)


def _samples_per_step(batch):
    return 2 if batch % 2 == 0 else 1


def _stream_tile(xa_ref, xb_ref, jj, nct):
    return jnp.where(jj < nct, xa_ref[...], xb_ref[...])


def _stream_specs(stream, tb, joff, nct):
    xa, xb, lat_off = stream
    d = xa.shape[2]
    return [pl.BlockSpec((tb, TM, d), lambda bi, j: (bi, jnp.minimum(j + joff, nct - 1), 0)),
            pl.BlockSpec((tb, TM, d), lambda bi, j: (bi, jnp.maximum(j + joff - nct, 0) + lat_off, 0))]


def _put_rows(ref, lanes, val):
    for u in range(ref.shape[0]):
        ref[u, :, lanes] = val[u * TM:(u + 1) * TM]


def _proj_kernel(xa_ref, xb_ref, mod_ref, win_ref, cm_ref, sm_ref, cd_ref, sd_ref, cg_ref, sg_ref,
                 qn_ref, kvn_ref, wq_ref, wkv_ref, gq_ref, gk_ref,
                 gb_ref, zc_ref, qm_ref, km_ref, vm_ref, qd_ref, kd_ref, vd_ref, qg_ref, kg_ref, vg_ref, *, nct):
    x = _stream_tile(xa_ref, xb_ref, pl.program_id(1), nct)
    tb, _, d = x.shape
    sh1 = mod_ref[:, 0, 0:1, :]
    sc1 = mod_ref[:, 0, 1:2, :]
    xm = (x * (1.0 + sc1) + sh1).reshape(tb * TM, d).astype(BF16)
    everything = slice(None)

    def table(ref):
        return jnp.concatenate([ref[...]] * tb, axis=0)

    uc = jnp.dot(xm, win_ref[:, P_CONV:P_MLA], preferred_element_type=F32)
    _put_rows(gb_ref, everything, uc[:, 0:256])
    _put_rows(zc_ref, everything, uc[:, 256:512] * uc[:, 512:768])

    um = jnp.dot(xm, win_ref[:, P_MLA:P_DIFF], preferred_element_type=F32)
    cm, sm = table(cm_ref), table(sm_ref)
    cq = _rms(um[:, 0:256], qn_ref[...]).astype(BF16)
    qall = jnp.dot(cq, wq_ref[...], preferred_element_type=F32)
    ckv = _rms(um[:, 256:384], kvn_ref[...]).astype(BF16)
    kv = jnp.dot(ckv, wkv_ref[...], preferred_element_type=F32)
    kr = _rope(um[:, 384:512], cm, sm, 8)
    for h in range(MLA_HEADS):
        sl = slice(LANES * h, LANES * (h + 1))
        _put_rows(qm_ref, sl, (_rope(qall[:, sl], cm, sm, 8) * (MLA_SCALE * LOG2E)).astype(BF16))
        _put_rows(km_ref, sl, (kv[:, sl] + kr).astype(BF16))
        _put_rows(vm_ref, sl, (kv[:, 512 + LANES * h:512 + LANES * (h + 1)] + _ones_lane(h % 2)).astype(BF16))

    ud = jnp.dot(xm, win_ref[:, P_DIFF:P_GQA], preferred_element_type=F32)
    cd, sd = table(cd_ref), table(sd_ref)
    for c in range(2):
        sl = slice(LANES * c, LANES * (c + 1))
        _put_rows(qd_ref, sl, (_rope(ud[:, sl], cd, sd, 8) * (DIFF_SCALE * LOG2E)).astype(BF16))
        _put_rows(kd_ref, sl, _rope(ud[:, 256 + LANES * c:256 + LANES * (c + 1)], cd, sd, 8).astype(BF16))
    for h in range(DIFF_HEADS):
        sl = slice(LANES * h, LANES * (h + 1))
        _put_rows(vd_ref, sl, (ud[:, 512 + LANES * h:512 + LANES * (h + 1)] + _ones_lane(h % 2)).astype(BF16))

    ug = jnp.dot(xm, win_ref[:, P_GQA:P_END], preferred_element_type=F32)
    cg, sg = table(cg_ref), table(sg_ref)
    for c in range(2):
        sl = slice(LANES * c, LANES * (c + 1))
        _put_rows(qg_ref, sl,
                  (_rope(_seg_rms64(ug[:, sl], gq_ref[...]), cg, sg, 16) * (GQA_SCALE * LOG2E)).astype(BF16))
    _put_rows(kg_ref, everything, _rope(_seg_rms64(ug[:, 256:384], gk_ref[...]), cg, sg, 16).astype(BF16))
    for i in range(4):
        sl = slice(LANES * i, LANES * (i + 1))
        _put_rows(vg_ref, sl, (ug[:, 384 + LANES * i:384 + LANES * (i + 1)] + _ones_lane(i % 2)).astype(BF16))


def _proj(stream, t, modsel, win_p, tabs, qn, kvn, wq, wkv, gq, gk, nct):
    b, _, d = stream[0].shape
    nt = t // TM
    tb = 1

    def tile(width):
        return pl.BlockSpec((tb, TM, width), lambda bi, j: (bi, j, 0))

    def full(arr):
        return pl.BlockSpec(arr.shape, lambda bi, j: (0,) * arr.ndim)

    tab_spec = pl.BlockSpec((TM, LANES), lambda bi, j: (j, 0))
    widths = [(256, F32), (256, F32), (512, BF16), (512, BF16), (512, BF16), (256, BF16), (256, BF16),
              (512, BF16), (256, BF16), (128, BF16), (512, BF16)]
    return pl.pallas_call(
        functools.partial(_proj_kernel, nct=nct),
        grid=(b // tb, nt),
        in_specs=_stream_specs(stream, tb, 0, nct) + [
                  pl.BlockSpec((tb, 1, 6, d), lambda bi, j: (bi, jnp.minimum(j // nct, 1), 0, 0)),
                  full(win_p)] + [tab_spec] * 6 + [full(a) for a in (qn, kvn, wq, wkv, gq, gk)],
        out_specs=[tile(w) for w, _ in widths],
        out_shape=[jax.ShapeDtypeStruct((b, t, w), dt) for w, dt in widths],
        compiler_params=_cparams(("arbitrary", "arbitrary")),
        name="proj",
    )(stream[0], stream[1], modsel, win_p, *tabs, qn, kvn, wq, wkv, gq, gk)


def _scores(q_ref, k_ref, tk, chunk, mask):
    q = q_ref[0, :, LANES * chunk:LANES * (chunk + 1)]
    if mask is not None:
        lane = _lane(q.shape)
        q = jnp.where((lane >= mask[0]) & (lane < mask[1]), q, jnp.zeros_like(q))
    k = k_ref[0, 0:tk, LANES * chunk:LANES * (chunk + 1)]
    return lax.dot_general(q, k, (((1,), (1,)), ((), ())), preferred_element_type=F32)


def _attend(s, v, odd):
    p = jnp.exp2(s - jnp.max(s, axis=-1, keepdims=True)).astype(BF16)
    o = jnp.dot(p, v, preferred_element_type=F32)
    den = o[:, 0:1] if odd else o[:, 64:65]
    return o * (1.0 / den)


def _pair(even, odd):
    return jnp.where(_lane(even.shape) < 64, even, odd)


def _vblock(v_ref, tk, idx):
    return v_ref[0, 0:tk, LANES * idx:LANES * (idx + 1)]


def _one_ahead(n, score_fn, finish_fn):
    outs, nxt = [], score_fn(0)
    for h in range(n):
        cur = nxt
        if h + 1 < n:
            nxt = score_fn(h + 1)
        outs.append(finish_fn(h, cur))
    return outs


def _mla_heads(q_ref, k_ref, v_ref, tk, extra):
    o = _one_ahead(MLA_HEADS, lambda h: _scores(q_ref, k_ref, tk, h, None),
                   lambda h, s: _attend(s, _vblock(v_ref, tk, h), h % 2))
    return jnp.concatenate([_pair(o[0], o[1]), _pair(o[2], o[3])], axis=-1)


def _gqa_heads(q_ref, k_ref, v_ref, tk, extra):
    def scores(i):
        kvh, g = divmod(i, GQA_Q_HEADS // GQA_KV_HEADS)
        q = q_ref[0, :, LANES * g:LANES * (g + 1)]
        lane = _lane(q.shape)
        q = jnp.where((lane >= 64 * kvh) & (lane < 64 * kvh + 64), q, jnp.zeros_like(q))
        return lax.dot_general(q, k_ref[0, 0:tk, :], (((1,), (1,)), ((), ())), preferred_element_type=F32)

    o = _one_ahead(GQA_Q_HEADS, scores, lambda i, s: _attend(s, _vblock(v_ref, tk, i), i % 2))
    return jnp.concatenate([_pair(o[0], o[1]), _pair(o[2], o[3])], axis=-1)


def _diff_heads(lam_init, q_ref, k_ref, v_ref, tk, extra):
    dl_ref, subln_ref = extra
    dl = dl_ref[...]
    lam = (jnp.exp(jnp.sum(dl[0:1] * dl[1:2], axis=-1, keepdims=True))
           - jnp.exp(jnp.sum(dl[2:3] * dl[3:4], axis=-1, keepdims=True)) + lam_init)

    def head_scores(h):
        return [_scores(q_ref, k_ref, tk, (2 * h + m) // 4, (32 * ((2 * h + m) % 4), 32 * ((2 * h + m) % 4) + 32))
                for m in range(2)]

    def finish(h, pair):
        probs = []
        for s in pair:
            e = jnp.exp2(s - jnp.max(s, axis=-1, keepdims=True))
            probs.append((e, jnp.sum(e, axis=-1, keepdims=True)))
        inv_l0 = 1.0 / probs[0][1]
        a = probs[0][0] - probs[1][0] * (lam * probs[0][1] / probs[1][1])
        return jnp.dot(a.astype(BF16), _vblock(v_ref, tk, h), preferred_element_type=F32) * inv_l0

    d = _one_ahead(DIFF_HEADS, head_scores, finish)
    chunks = [_seg_rms64(_pair(d[2 * c], d[2 * c + 1]), subln_ref[...]) * (1.0 - lam_init) for c in range(2)]
    return jnp.concatenate(chunks, axis=-1)


def _attn_kernel(*refs, heads, joff, nct, ctx_len, total_len):
    q_ref, k_ref, v_ref = refs[:3]
    extra, o_ref = refs[3:-1], refs[-1]
    jj = pl.program_id(1) + joff

    def run(tk):
        o_ref[0] = heads(q_ref, k_ref, v_ref, tk, extra).astype(o_ref.dtype)

    if joff < nct:
        pl.when(jj < nct)(lambda: run(ctx_len))
        pl.when(jj >= nct)(lambda: run(total_len))
    else:
        run(total_len)


def _attention(q, k, v, extra, heads, joff, nct, name):
    b, t, _ = q.shape
    nt = t // TM

    def full(arr):
        return pl.BlockSpec(arr.shape, lambda bi, j: (0,) * arr.ndim)

    kern = functools.partial(_attn_kernel, heads=heads, joff=joff, nct=nct, ctx_len=nct * TM, total_len=t)
    return pl.pallas_call(
        kern,
        grid=(b, nt - joff),
        in_specs=[pl.BlockSpec((1, TM, q.shape[2]), lambda bi, j: (bi, j + joff, 0)),
                  pl.BlockSpec((1, t, k.shape[2]), lambda bi, j: (bi, 0, 0)),
                  pl.BlockSpec((1, t, v.shape[2]), lambda bi, j: (bi, 0, 0))] + [full(a) for a in extra],
        out_specs=pl.BlockSpec((1, TM, 256), lambda bi, j: (bi, j, 0)),
        out_shape=jax.ShapeDtypeStruct((b, t - joff * TM, 256), BF16),
        compiler_params=_cparams(("arbitrary", "arbitrary")),
        name=name,
    )(q, k, v, *extra)


def _layer_norm(h, g, bias):
    mu = jnp.mean(h, axis=-1, keepdims=True)
    hc = h - mu
    var = jnp.mean(hc * hc, axis=-1, keepdims=True)
    return hc * lax.rsqrt(var + NORM_EPS) * g + bias


def _store_token_tiles(ref, val):
    rows = val.shape[0]
    for c in range(SUBLANES):
        ref[pl.ds(c, rows, stride=SUBLANES), :] = val[:, LANES * c:LANES * (c + 1)]


def _load_token_tiles(ref, rows):
    return jnp.concatenate([ref[pl.ds(c, rows, stride=SUBLANES), :] for c in range(SUBLANES)], axis=1)


def _min_lane(cond, lane_f):
    return jnp.min(jnp.where(cond, lane_f, float(LANES)), axis=-1, keepdims=True)


def _post_kernel(xa_ref, xb_ref, mod_ref, gb_ref, zc_ref, zp_ref, zn_ref, ym_ref, yd_ref, yg_ref, cw_ref, wout_ref,
                 g_ref, b_ref, wr_ref, br_ref, x1_ref, tok_ref, route_ref, cnt_ref, cnt_acc,
                 *, joff, nct, nt, alpha):
    jj = pl.program_id(1) + joff
    first_step = (pl.program_id(0) == 0) & (pl.program_id(1) == 0)

    @pl.when(first_step)
    def _():
        cnt_acc[...] = jnp.zeros_like(cnt_acc)

    tb = zc_ref.shape[0]
    rows = tb * TM
    row = lax.broadcasted_iota(jnp.int32, (TM, zc_ref.shape[2]), 0)
    left_ok = (jj != 0) & (jj != nct)
    right_ok = (jj != nct - 1) & (jj != nt - 1)
    cw = cw_ref[...]
    mixed = []
    for u in range(tb):
        zc = zc_ref[u]
        halo_prev = jnp.where(left_ok, zp_ref[u, SUBLANES - 1:SUBLANES, :], 0.0)
        halo_next = jnp.where(right_ok, zn_ref[u, 0:1, :], 0.0)
        zprev = jnp.where(row == 0, halo_prev, pltpu.roll(zc, 1, 0))
        znext = jnp.where(row == TM - 1, halo_next, pltpu.roll(zc, TM - 1, 0))
        conv = zprev * cw[0:1] + zc * cw[1:2] + znext * cw[2:3]
        mixed.append(jnp.concatenate([(gb_ref[u] * conv).astype(BF16), ym_ref[u], yd_ref[u], yg_ref[u]], axis=-1))
    y = jnp.concatenate(mixed, axis=0)
    acc = jnp.dot(y, wout_ref[...], preferred_element_type=F32)
    d = acc.shape[1]

    g1 = mod_ref[:, 0, 2:3, :]
    sh2 = mod_ref[:, 0, 3:4, :]
    sc2 = mod_ref[:, 0, 4:5, :]
    x1 = _layer_norm(alpha * _stream_tile(xa_ref, xb_ref, jj, nct) + g1 * acc.reshape(tb, TM, d),
                     g_ref[...], b_ref[...])
    x1_ref[...] = x1
    tok3 = x1 * (1.0 + sc2) + sh2
    for u in range(tb):
        _store_token_tiles(tok_ref.at[u], tok3[u])
    tok = tok3.reshape(rows, d)

    tok_hi = tok.astype(BF16)
    tok_lo = (tok - tok_hi.astype(F32)).astype(BF16)
    part = jnp.dot(tok_hi, wr_ref[...], preferred_element_type=F32)
    logits = (part[:, 0:LANES] + (part[:, LANES:2 * LANES]
                                  + jnp.dot(tok_lo, wr_ref[:, 0:LANES], preferred_element_type=F32))) + br_ref[...]
    lane = _lane(logits.shape)
    lane_f = lane.astype(F32)
    neg = -jnp.inf
    is_g = (lane >= N_EXPERTS) & (lane < N_EXPERTS + N_GROUPS)
    lg = jnp.where(is_g, logits, neg)
    eg = jnp.exp(lg - jnp.max(lg, axis=-1, keepdims=True))
    pg = eg / jnp.sum(eg, axis=-1, keepdims=True)
    g_w = jnp.max(pg, axis=-1, keepdims=True)
    g_lane = _min_lane(is_g & (pg == g_w), lane_f)
    g_idx = g_lane - float(N_EXPERTS)
    in_grp = (lane < N_EXPERTS) & ((lane // EXPERTS_PER_GROUP).astype(F32) == g_idx)
    ls = jnp.where(in_grp, logits, neg)
    es = jnp.exp(ls - jnp.max(ls, axis=-1, keepdims=True))
    ps = es / jnp.sum(es, axis=-1, keepdims=True)
    p1 = jnp.max(jnp.where(in_grp, ps, -1.0), axis=-1, keepdims=True)
    i1 = _min_lane(in_grp & (ps == p1), lane_f)
    rest = in_grp & (lane_f != i1)
    p2 = jnp.max(jnp.where(rest, ps, -1.0), axis=-1, keepdims=True)
    i2 = _min_lane(rest & (ps == p2), lane_f)
    tot = p1 + p2
    w1 = g_w * (p1 / tot)
    w2 = g_w * (p2 / tot)

    oh1 = lane_f == i1
    oh2 = lane_f == i2
    r_i = lax.broadcasted_iota(jnp.int32, (rows, rows), 0)
    c_i = lax.broadcasted_iota(jnp.int32, (rows, rows), 1)
    tri = (r_i > c_i).astype(BF16)
    before1 = jnp.dot(tri, oh1.astype(BF16), preferred_element_type=F32)
    before2 = jnp.dot(tri, oh2.astype(BF16), preferred_element_type=F32)
    tot1 = jnp.sum(oh1.astype(F32), axis=0, keepdims=True)
    tot2 = jnp.sum(oh2.astype(F32), axis=0, keepdims=True)
    base = cnt_acc[...]
    r1 = jnp.sum(jnp.where(oh1, base + before1, 0.0), axis=-1, keepdims=True)
    r2 = jnp.sum(jnp.where(oh2, base + tot1 + before2, 0.0), axis=-1, keepdims=True)
    new_cnt = base + tot1 + tot2
    cnt_acc[...] = new_cnt
    cnt_ref[...] = jnp.broadcast_to(new_cnt, cnt_ref.shape)

    vals = (i1, i2, w1, w2, r1, r2)
    route = jnp.zeros(logits.shape, F32)
    for idx, val in enumerate(vals):
        route = jnp.where(lane == idx, val, route)
    _put_rows(route_ref, slice(None), route)


def _post(stream, t, modsel, gb, zc, ym, yd, yg, conv_w, wout, ln_g, ln_b, wr, br, joff, nct, alpha):
    b, _, d = stream[0].shape
    nt = t // TM
    nj = nt - joff
    hb = TM // SUBLANES
    tb = _samples_per_step(b)

    def tile(width):
        return pl.BlockSpec((tb, TM, width), lambda bi, j: (bi, j + joff, 0))

    def otile(width):
        return pl.BlockSpec((tb, TM, width), lambda bi, j: (bi, j, 0))

    def full(arr):
        return pl.BlockSpec(arr.shape, lambda bi, j: (0,) * arr.ndim)

    kern = functools.partial(_post_kernel, joff=joff, nct=nct, nt=nt, alpha=alpha)
    return pl.pallas_call(
        kern,
        grid=(b // tb, nj),
        in_specs=_stream_specs(stream, tb, joff, nct) + [
                  pl.BlockSpec((tb, 1, 6, d), lambda bi, j: (bi, jnp.minimum((j + joff) // nct, 1), 0, 0)),
                  tile(256), tile(256),
                  pl.BlockSpec((tb, SUBLANES, 256), lambda bi, j: (bi, jnp.maximum((j + joff) * hb - 1, 0), 0)),
                  pl.BlockSpec((tb, SUBLANES, 256),
                               lambda bi, j: (bi, jnp.minimum((j + joff + 1) * hb, nt * hb - 1), 0)),
                  otile(256), otile(256), otile(256),
                  full(conv_w), full(wout), full(ln_g), full(ln_b), full(wr), full(br)],
        out_specs=[otile(d), pl.BlockSpec((tb, TM * SUBLANES, LANES), lambda bi, j: (bi, j, 0)),
                   otile(LANES), pl.BlockSpec((SUBLANES, LANES), lambda bi, j: (0, 0))],
        out_shape=[jax.ShapeDtypeStruct((b, nj * TM, d), F32),
                   jax.ShapeDtypeStruct((b, nj * TM * SUBLANES, LANES), F32),
                   jax.ShapeDtypeStruct((b, nj * TM, LANES), F32),
                   jax.ShapeDtypeStruct((SUBLANES, LANES), F32)],
        scratch_shapes=[pltpu.VMEM((1, LANES), F32)],
        compiler_params=_cparams(("arbitrary", "arbitrary")),
        name="post",
    )(stream[0], stream[1], modsel, gb, zc, zc, zc, ym, yd, yg, conv_w, wout, ln_g, ln_b, wr, br)


ROW_UNROLL = EB
TILE_ROWS = SUBLANES


def _expert_kernel(be_ref, offs_ref, tok_ref, w1_ref, w3_ref, w2_ref, out_ref, xbuf, ybuf, gsem, ssem,
                   *, n_tok, nblk):
    del be_ref
    i = pl.program_id(0)
    slot = i % 2
    blk_rows = EB * TILE_ROWS

    def tile_at(ref, first_row):
        return ref.at[pl.ds(pl.multiple_of(first_row, TILE_ROWS), TILE_ROWS)]

    def gather_start(blk, sl):
        def body(g, carry):
            for u in range(ROW_UNROLL):
                r = g * ROW_UNROLL + u
                pltpu.make_async_copy(tile_at(tok_ref, offs_ref[blk, r]), tile_at(xbuf.at[sl], r * TILE_ROWS),
                                      gsem.at[sl]).start(priority=u % 2)
            return carry
        lax.fori_loop(0, EB // ROW_UNROLL, body, 0)

    def scatter_start(blk, sl):
        def body(g, carry):
            for u in range(ROW_UNROLL):
                r = g * ROW_UNROLL + u
                pltpu.make_async_copy(tile_at(ybuf.at[sl], r * TILE_ROWS), tile_at(out_ref, offs_ref[blk, EB + r]),
                                      ssem.at[sl]).start(priority=u % 2)
            return carry
        lax.fori_loop(0, EB // ROW_UNROLL, body, 0)

    def gather_wait(sl):
        pltpu.make_async_copy(tok_ref.at[pl.ds(0, blk_rows)], xbuf.at[sl], gsem.at[sl]).wait()

    def scatter_wait(sl):
        pltpu.make_async_copy(ybuf.at[sl], out_ref.at[pl.ds(0, blk_rows)], ssem.at[sl]).wait()

    @pl.when(i == 0)
    def _():
        gather_start(0, 0)

    @pl.when(i + 1 < nblk)
    def _():
        gather_start(i + 1, 1 - slot)

    gather_wait(slot)

    @pl.when(i >= 2)
    def _():
        scatter_wait(slot)

    xb = _load_token_tiles(xbuf.at[slot], EB).astype(BF16)
    h1 = jnp.dot(xb, w1_ref[0, 0].astype(BF16), preferred_element_type=F32)
    h3 = jnp.dot(xb, w3_ref[0, 0].astype(BF16), preferred_element_type=F32)
    a = (h1 * jax.nn.sigmoid(h1) * h3).astype(BF16)
    _store_token_tiles(ybuf.at[slot], jnp.dot(a, w2_ref[0, 0].astype(BF16), preferred_element_type=F32))
    scatter_start(i, slot)

    @pl.when(i == nblk - 1)
    def _():
        scatter_wait(1 - slot)
        scatter_wait(slot)
        ybuf[0] = jnp.zeros(ybuf.shape[1:], F32)
        for half in range(2):
            cp = pltpu.make_async_copy(
                ybuf.at[0], out_ref.at[pl.ds((2 * n_tok + half * EB) * TILE_ROWS, blk_rows)], ssem.at[0])
            cp.start()
            cp.wait()


def _experts(block_e, offs, tok_tiles, w1, w3, w2, layer):
    n_tok = tok_tiles.shape[0] // TILE_ROWS
    d, de = w1.shape[2], w1.shape[3]
    nblk = offs.shape[0]
    assert nblk >= 2 and d == TILE_ROWS * LANES
    kern = functools.partial(_expert_kernel, n_tok=n_tok, nblk=nblk)
    return pl.pallas_call(
        kern,
        grid_spec=pltpu.PrefetchScalarGridSpec(
            num_scalar_prefetch=2,
            grid=(nblk,),
            in_specs=[pl.BlockSpec(memory_space=pl.ANY),
                      pl.BlockSpec((1, 1, d, de), lambda i, be, offs: (layer, be[i], 0, 0)),
                      pl.BlockSpec((1, 1, d, de), lambda i, be, offs: (layer, be[i], 0, 0)),
                      pl.BlockSpec((1, 1, de, d), lambda i, be, offs: (layer, be[i], 0, 0))],
            out_specs=pl.BlockSpec(memory_space=pl.ANY),
            scratch_shapes=[pltpu.VMEM((2, EB * TILE_ROWS, LANES), F32), pltpu.VMEM((2, EB * TILE_ROWS, LANES), F32),
                            pltpu.SemaphoreType.DMA((2,)), pltpu.SemaphoreType.DMA((2,))]),
        out_shape=jax.ShapeDtypeStruct(((2 * n_tok + 2 * EB) * TILE_ROWS, LANES), F32),
        compiler_params=_cparams(("arbitrary",)),
        name="experts",
    )(block_e, offs, tok_tiles, w1, w3, w2)


def _combine_kernel(x1_ref, mod_ref, route_ref, *rest, alpha):
    tb = x1_ref.shape[0]
    f_refs, (g_ref, b_ref, o_ref) = rest[:2 * tb], rest[2 * tb:]
    for u in range(tb):
        route = route_ref[u]
        f = (route[:, 2:3] * _load_token_tiles(f_refs[2 * u], TM)
             + route[:, 3:4] * _load_token_tiles(f_refs[2 * u + 1], TM))
        g2 = mod_ref[u, 0, 5:6, :]
        o_ref[u] = _layer_norm(alpha * x1_ref[u] + g2 * f, g_ref[...], b_ref[...])


def _combine(x1, modsel, route, ys, ln_g, ln_b, joff, nct, alpha):
    b, n, d = x1.shape
    nj = n // TM
    plane = b * nj
    tb = _samples_per_step(b)

    def full(arr):
        return pl.BlockSpec(arr.shape, lambda bi, j: (0,) * arr.ndim)

    f_specs = []
    for u in range(tb):
        for k in range(2):
            f_specs.append(pl.BlockSpec((TM * TILE_ROWS, LANES),
                                        lambda bi, j, u=u, k=k: (k * plane + (bi * tb + u) * nj + j, 0)))
    return pl.pallas_call(
        functools.partial(_combine_kernel, alpha=alpha),
        grid=(b // tb, nj),
        in_specs=[pl.BlockSpec((tb, TM, d), lambda bi, j: (bi, j, 0)),
                  pl.BlockSpec((tb, 1, 6, d), lambda bi, j: (bi, jnp.minimum((j + joff) // nct, 1), 0, 0)),
                  pl.BlockSpec((tb, TM, LANES), lambda bi, j: (bi, j, 0))] + f_specs + [full(ln_g), full(ln_b)],
        out_specs=pl.BlockSpec((tb, TM, d), lambda bi, j: (bi, j, 0)),
        out_shape=jax.ShapeDtypeStruct((b, n, d), F32),
        compiler_params=_cparams(("arbitrary", "arbitrary")),
        name="combine",
    )(x1, modsel, route, *([ys] * (2 * tb)), ln_g, ln_b)


def _rope_tables(seq, ctx_len):
    t = jnp.arange(seq, dtype=jnp.int32)
    row = (t // GRID_W).astype(F32)
    col = (t % GRID_W).astype(F32)

    def table(vec_dim):
        half = vec_dim // 4
        freqs = ROPE_THETA ** (-jnp.arange(half, dtype=F32) / half)
        lane = np.arange(vec_dim)
        idx = lane % half
        use_col = (lane // (2 * half)) == 1
        ang = jnp.where(use_col[None, :], col[:, None], row[:, None]) * freqs[idx][None, :]
        sign = np.where((lane % (2 * half)) < half, -1.0, 1.0).astype(np.float32)
        return jnp.cos(ang), jnp.sin(ang) * sign[None, :]

    def with_ctx(cos, sin):
        width = cos.shape[1]
        return (jnp.concatenate([jnp.ones((ctx_len, width), F32), cos], axis=0),
                jnp.concatenate([jnp.zeros((ctx_len, width), F32), sin], axis=0))

    c32, s32 = table(32)
    c64, s64 = table(64)
    cd, sd = with_ctx(jnp.tile(c32, (1, 4)), jnp.tile(s32, (1, 4)))
    cg, sg = with_ctx(jnp.tile(c64, (1, 2)), jnp.tile(s64, (1, 2)))
    ones, zeros = jnp.ones((seq, 64), F32), jnp.zeros((seq, 64), F32)
    cm, sm = with_ctx(jnp.concatenate([ones, c32, ones[:, :32]], axis=1),
                      jnp.concatenate([zeros, s32, zeros[:, :32]], axis=1))
    return cm, sm, cd, sd, cg, sg


def _in_proj_columns():
    def spread(start, odd_blocks):
        cols = []
        for i, odd in enumerate(odd_blocks):
            vals = list(range(start + 64 * i, start + 64 * (i + 1)))
            cols += ([-1] * 64 + vals) if odd else (vals + [-1] * 64)
        return cols

    cols = list(range(0, IN_CONV))
    cols += list(range(OFF_MLA, OFF_MLA + MLA_Q_LORA + MLA_KV_LORA))
    cols += [-1] * 64 + list(range(OFF_MLA + MLA_Q_LORA + MLA_KV_LORA, OFF_DIFF)) + [-1] * 32
    cols += list(range(OFF_DIFF, OFF_DIFF + 2 * DIFF_QW))
    cols += spread(OFF_DIFF + 2 * DIFF_QW, [h % 2 for h in range(DIFF_HEADS)])
    q0 = OFF_GQA
    head = lambda h: list(range(q0 + GQA_HD * h, q0 + GQA_HD * (h + 1)))
    cols += head(0) + head(2) + head(1) + head(3)
    cols += list(range(OFF_GQA + GQA_QW, OFF_GQA + GQA_QW + GQA_KW))
    v0 = OFF_GQA + GQA_QW + GQA_KW
    for kvh in range(GQA_KV_HEADS):
        vals = list(range(v0 + GQA_HD * kvh, v0 + GQA_HD * (kvh + 1)))
        cols += vals + [-1] * 64 + [-1] * 64 + vals
    assert len(cols) == P_END
    return np.asarray(cols, np.int32)


def _relayout_columns(w, cols):
    valid = jnp.asarray(cols >= 0)
    return jnp.where(valid[None, :], jnp.take(w, jnp.asarray(np.maximum(cols, 0)), axis=1), 0.0)


def _mla_weights(w_uq, w_qr, w_uk, w_uv):
    zq = jnp.zeros((MLA_Q_LORA, 32), F32)
    zk = jnp.zeros((MLA_KV_LORA, 64), F32)
    wq = jnp.concatenate([blk for h in range(MLA_HEADS)
                          for blk in (w_uq[:, 64 * h:64 * (h + 1)], w_qr[:, 32 * h:32 * (h + 1)], zq)], axis=1)
    wk = jnp.concatenate([blk for h in range(MLA_HEADS) for blk in (w_uk[:, 64 * h:64 * (h + 1)], zk)], axis=1)
    wv = jnp.concatenate([blk for h in range(MLA_HEADS)
                          for blk in ((zk, w_uv[:, 64 * h:64 * (h + 1)]) if h % 2 else
                                      (w_uv[:, 64 * h:64 * (h + 1)], zk))], axis=1)
    return wq.astype(BF16), jnp.concatenate([wk, wv], axis=1).astype(BF16)


def _slot_plan(route, counts, n_tok):
    n_slots = 2 * n_tok + N_EXPERTS * EB
    nblk = n_slots // EB
    cnt = counts.astype(jnp.int32)
    padded = (cnt + EB - 1) // EB * EB
    pad_ends = jnp.cumsum(padded)
    pad_starts = pad_ends - padded
    cols = route[:, 0:SUBLANES].T.astype(jnp.int32)
    dest = jnp.concatenate([pad_starts[cols[0]] + cols[4], pad_starts[cols[1]] + cols[5]])
    slot = jnp.arange(n_slots, dtype=jnp.int32)
    discard = 2 * n_tok + ((slot // EB) % 2) * EB + slot % EB
    res_row = discard.at[dest].set(jnp.arange(2 * n_tok, dtype=jnp.int32))
    src_row = res_row - n_tok * ((res_row >= n_tok).astype(jnp.int32) + (res_row >= 2 * n_tok).astype(jnp.int32))
    offs = jnp.concatenate([src_row.reshape(nblk, EB), res_row.reshape(nblk, EB)], axis=1) * TILE_ROWS
    block_start = jnp.arange(nblk, dtype=jnp.int32) * EB
    block_e = jnp.minimum(jnp.sum((pad_ends[None, :] <= block_start[:, None]).astype(jnp.int32), axis=1),
                          N_EXPERTS - 1)
    return block_e, offs


def kernel(x, c, ctx, c_ctx, w_ada, b_ada, w_in, w_out, conv_w, mla_q_norm, mla_kv_norm, mla_w_uq, mla_w_qr,
           mla_w_uk, mla_w_uv, diff_lambda, diff_subln, gqa_q_norm, gqa_k_norm, ln1_g, ln1_b, ln2_g, ln2_b,
           moe_w_group, moe_b_group, moe_w_sub, moe_b_sub, moe_w1, moe_w3, moe_w2):
    b, s, d = x.shape
    n_ctx = ctx.shape[1]
    depth = w_in.shape[0]
    assert n_ctx % TM == 0 and s % TM == 0 and s % GRID_W == 0 and d == SUBLANES * LANES
    nct = n_ctx // TM
    t = n_ctx + s
    alpha = (2 * depth) ** 0.25

    rows = ((b + 1 + SUBLANES - 1) // SUBLANES) * SUBLANES
    cc = jnp.zeros((rows, d), F32).at[:b].set(c).at[b].set(c_ctx)
    mod = _modulation(cc, w_ada, b_ada)
    tabs = _rope_tables(s, n_ctx)
    cols = _in_proj_columns()
    stream = (ctx, x, 0)

    for l in range(depth):
        last = l == depth - 1
        joff = nct if last else 0
        lam_init = 0.8 - 0.6 * math.exp(-0.3 * l)
        ml = mod[l].reshape(rows, 6, d)
        modsel = jnp.stack([jnp.broadcast_to(ml[b], (b, 6, d)), ml[:b]], axis=1)
        win_p = _relayout_columns(w_in[l], cols).astype(BF16)
        wq, wkv = _mla_weights(mla_w_uq[l], mla_w_qr[l], mla_w_uk[l], mla_w_uv[l])
        gq = jnp.tile(gqa_q_norm[l], 2)[None, :]
        gk = jnp.tile(gqa_k_norm[l], 2)[None, :]
        gb, zc, qm, km, vm, qd, kd, vd, qg, kg, vg = _proj(
            stream, t, modsel, win_p, tabs, mla_q_norm[l][None, :], mla_kv_norm[l][None, :], wq, wkv, gq, gk, nct)

        ym = _attention(qm, km, vm, (), _mla_heads, joff, nct, "attn_mla")
        yd = _attention(qd, kd, vd, (diff_lambda[l], jnp.tile(diff_subln[l], 2)[None, :]),
                        functools.partial(_diff_heads, lam_init), joff, nct, "attn_diff")
        yg = _attention(qg, kg, vg, (), _gqa_heads, joff, nct, "attn_gqa")

        wr = jnp.zeros((d, LANES), F32).at[:, :N_EXPERTS].set(moe_w_sub[l])
        wr = wr.at[:, N_EXPERTS:N_EXPERTS + N_GROUPS].set(moe_w_group[l])
        wr_hi = wr.astype(BF16)
        wr = jnp.concatenate([wr_hi, (wr - wr_hi.astype(F32)).astype(BF16)], axis=1)
        br = jnp.zeros((1, LANES), F32).at[0, :N_EXPERTS].set(moe_b_sub[l])
        br = br.at[0, N_EXPERTS:N_EXPERTS + N_GROUPS].set(moe_b_group[l])
        x1, tok, route, counts = _post(stream, t, modsel, gb, zc, ym, yd, yg, conv_w[l], w_out[l].astype(BF16),
                                       ln1_g[l][None, :], ln1_b[l][None, :], wr, br, joff, nct, alpha)

        n_tok = x1.shape[0] * x1.shape[1]
        block_e, offs = _slot_plan(route.reshape(n_tok, LANES), counts[0, :N_EXPERTS], n_tok)
        ys = _experts(block_e, offs, tok.reshape(n_tok * SUBLANES, LANES), moe_w1, moe_w3, moe_w2, l)
        xnext = _combine(x1, modsel, route, ys, ln2_g[l][None, :], ln2_b[l][None, :], joff, nct, alpha)
        stream = (xnext, xnext, nct)
    return xnext
```
